```python
import jax
import jax.numpy as jnp
from jax import lax
import numpy as np

D_MODEL = 1024
BATCH = 32
SEQ = 2048
DEPTH = 2

GRID_W = 64
CTX_LEN = 256
EPS = 1e-6
ROPE_BASE = 10000.0
NEG_INF = -1e30
N_ADA = 6
MIX_WIDTH = D_MODEL
RET_HEADS = 4
RET_V_DIM = MIX_WIDTH // (2 * RET_HEADS)
RET_QK_DIM = RET_V_DIM // 2
RET_CHUNK = 128
CONV_CH = MIX_WIDTH // 2
CONV_K = 3
ATT_HEADS = 16
ATT_HEAD_DIM = MIX_WIDTH // ATT_HEADS
ATT_KV_HEADS = 4
ATT_GROUP = ATT_HEADS // ATT_KV_HEADS
WINDOW = 128
ATT_BLOCK = 128
PEER_HEADS = 8
PEER_N_KEYS = 128
PEER_N_EXPERTS = PEER_N_KEYS * PEER_N_KEYS
PEER_D_KEY = 128
PEER_TOPK = 16
PEER_TOKEN_BLOCK = 128
N_EVEN = (DEPTH + 1) // 2
N_ODD = DEPTH // 2
RET_Q_W = RET_HEADS * RET_QK_DIM
RET_V_W = RET_HEADS * RET_V_DIM
EV_SPLITS = (RET_Q_W, 2 * RET_Q_W, 2 * RET_Q_W + RET_V_W, 2 * RET_Q_W + 2 * RET_V_W,
             2 * RET_Q_W + 2 * RET_V_W + CONV_CH, 2 * RET_Q_W + 2 * RET_V_W + 2 * CONV_CH)
EV_IN_COLS = 2 * RET_Q_W + 2 * RET_V_W + 3 * CONV_CH
ATT_Q_W = ATT_HEADS * ATT_HEAD_DIM
ATT_KV_W = ATT_KV_HEADS * ATT_HEAD_DIM
OD_IN_COLS = ATT_Q_W + 2 * ATT_KV_W

kernel_name = 'hybrid_retention_shortconv_swa_peer_dit'


def rmsnorm(x, gain):
    x32 = x.astype(jnp.float32)
    y = x32 * lax.rsqrt(jnp.mean(x32 * x32, axis=-1, keepdims=True) + EPS)
    return (y * gain.astype(jnp.float32)).astype(x.dtype)


def head_rmsnorm(y):
    y32 = y.astype(jnp.float32)
    return y32 * lax.rsqrt(jnp.mean(y32 * y32, axis=-1, keepdims=True) + EPS)


def modulate(x, shift, scale):
    return x * (1.0 + scale) + shift


def axial_rope_tables(n_tok, head_dim):
    n_rows = n_tok // GRID_W
    rows = jnp.broadcast_to(jnp.arange(n_rows, dtype=jnp.float32)[:, None], (n_rows, GRID_W)).reshape(-1)
    cols = jnp.broadcast_to(jnp.arange(GRID_W, dtype=jnp.float32)[None, :], (n_rows, GRID_W)).reshape(-1)
    n_freq = head_dim // 4
    inv_freq = ROPE_BASE ** (-jnp.arange(n_freq, dtype=jnp.float32) / n_freq)
    ang = jnp.concatenate([rows[:, None] * inv_freq, cols[:, None] * inv_freq], axis=-1)
    return jnp.cos(ang), jnp.sin(ang)


def _rotate_half(u, cos, sin):
    n = u.shape[-1] // 2
    u1, u2 = u[..., :n], u[..., n:]
    return jnp.concatenate([u1 * cos - u2 * sin, u2 * cos + u1 * sin], axis=-1)


def apply_axial_rope(x, cos, sin):
    n = x.shape[-1] // 4
    c = cos[None, :, None, :]
    s = sin[None, :, None, :]
    xr = _rotate_half(x[..., :2 * n], c[..., :n], s[..., :n])
    xc = _rotate_half(x[..., 2 * n:], c[..., n:], s[..., n:])
    return jnp.concatenate([xr, xc], axis=-1).astype(x.dtype)


def short_conv(u, w):
    t = u.shape[1]
    pad = CONV_K // 2
    up = jnp.pad(u, ((0, 0), (pad, pad), (0, 0)))
    return sum(w[i] * up[:, i:i + t] for i in range(CONV_K))


def retention_chunked(q, k, v, log_gamma, state0):
    b, t, h, dk = q.shape
    dv = v.shape[-1]
    n = t // RET_CHUNK
    qc = q.reshape(b, n, RET_CHUNK, h, dk)
    kc = k.reshape(b, n, RET_CHUNK, h, dk)
    vc = v.reshape(b, n, RET_CHUNK, h, dv)
    pos = jnp.arange(RET_CHUNK, dtype=jnp.float32)
    diff = pos[:, None] - pos[None, :]
    decay_in = jnp.where(diff[None] >= 0,
                         jnp.exp(log_gamma[:, None, None] * jnp.maximum(diff, 0.0)[None]), 0.0)
    scores = jnp.einsum('bnihd,bnjhd->bnhij', qc, kc) * decay_in
    inner = jnp.einsum('bnhij,bnjhe->bnihe', scores, vc)
    zeta = jnp.exp(log_gamma[:, None] * (RET_CHUNK - 1 - pos)[None])
    kv = jnp.einsum('bnjhd,bnjhe,hj->nbhde', kc, vc, zeta)
    chunk_decay = jnp.exp(log_gamma * RET_CHUNK)[None, :, None, None]

    def step(state, kv_i):
        return chunk_decay * state + kv_i, state

    state_final, state_prev = lax.scan(step, state0, kv)
    xi = jnp.exp(log_gamma[None, :] * (pos[:, None] + 1.0))
    cross = jnp.einsum('bnihd,nbhde->bnihe', qc, state_prev) * xi[None, None, :, :, None]
    return (inner + cross).reshape(b, t, h, dv), state_final


def retention_final_state(k, v, log_gamma):
    t = k.shape[1]
    pos = jnp.arange(t, dtype=jnp.float32)
    w = jnp.exp(log_gamma[:, None] * (t - 1 - pos)[None])
    return jnp.einsum('bthd,bthe,ht->bhde', k, v, w)


def bidirectional_retention(q_l, k_l, v_l, q_c, k_c, v_c, log_f, log_b):
    b, _, h, dk = k_c.shape
    dv = v_c.shape[-1]
    flip = lambda a: a[:, ::-1]
    zeros = jnp.zeros((b, h, dk, dv), jnp.float32)
    if q_c is not None:
        ctx_f, s_f = retention_chunked(q_c, k_c, v_c, log_f, zeros)
        ctx_b, s_b = retention_chunked(flip(q_c), flip(k_c), flip(v_c), log_b, zeros)
        out_ctx = ctx_f + flip(ctx_b)
    else:
        s_f = retention_final_state(k_c, v_c, log_f)
        s_b = retention_final_state(flip(k_c), flip(v_c), log_b)
        out_ctx = None
    lat_f, _ = retention_chunked(q_l, k_l, v_l, log_f, s_f)
    lat_b, _ = retention_chunked(flip(q_l), flip(k_l), flip(v_l), log_b, s_b)
    return lat_f + flip(lat_b), out_ctx


def even_output(ret, g, gb, gc, xt, conv_w, w_out):
    b, t = ret.shape[:2]
    y_ret = (head_rmsnorm(ret).reshape(b, t, RET_V_W) * jax.nn.silu(g.astype(jnp.float32))).astype(g.dtype)
    y_conv = gb * short_conv(gc * xt, conv_w)
    return jnp.concatenate([y_ret, y_conv], axis=-1) @ w_out


def even_mixer(a_lat, a_ctx, w_in, w_out, logit_f, logit_b, conv_w, need_ctx):
    b, n_lat, _ = a_lat.shape
    n_ctx = a_ctx.shape[1]
    k_scale = RET_QK_DIM ** -0.5
    log_f = jax.nn.log_sigmoid(logit_f.astype(jnp.float32))
    log_b = jax.nn.log_sigmoid(logit_b.astype(jnp.float32))
    cos, sin = axial_rope_tables(n_lat, RET_QK_DIM)
    q_l, k_l, v_l, g_l, gb_l, gc_l, x_l = jnp.split(a_lat @ w_in, EV_SPLITS, axis=-1)
    q_l = apply_axial_rope(q_l.reshape(b, n_lat, RET_HEADS, RET_QK_DIM), cos, sin)
    k_l = apply_axial_rope(k_l.reshape(b, n_lat, RET_HEADS, RET_QK_DIM), cos, sin) * k_scale
    v_l = v_l.reshape(b, n_lat, RET_HEADS, RET_V_DIM)
    if need_ctx:
        q_c, k_c, v_c, g_c, gb_c, gc_c, x_c = jnp.split(a_ctx @ w_in, EV_SPLITS, axis=-1)
        q_c = q_c.reshape(b, n_ctx, RET_HEADS, RET_QK_DIM)
    else:
        k_c, v_c = jnp.split(a_ctx @ w_in[:, RET_Q_W:2 * RET_Q_W + RET_V_W], (RET_Q_W,), axis=-1)
        q_c = None
    k_c = k_c.reshape(b, n_ctx, RET_HEADS, RET_QK_DIM) * k_scale
    v_c = v_c.reshape(b, n_ctx, RET_HEADS, RET_V_DIM)
    ret_l, ret_c = bidirectional_retention(q_l, k_l, v_l, q_c, k_c, v_c, log_f, log_b)
    out_lat = even_output(ret_l, g_l, gb_l, gc_l, x_l, conv_w, w_out)
    out_ctx = even_output(ret_c, g_c, gb_c, gc_c, x_c, conv_w, w_out) if need_ctx else None
    return out_lat, out_ctx


def odd_mixer(a_lat, a_ctx, w_in, w_out, sinks, need_ctx):
    b, n_lat, _ = a_lat.shape
    n_ctx = a_ctx.shape[1]
    scale = ATT_HEAD_DIM ** -0.5
    cos, sin = axial_rope_tables(n_lat, ATT_HEAD_DIM)
    q, k, v = jnp.split(a_lat @ w_in, (ATT_Q_W, ATT_Q_W + ATT_KV_W), axis=-1)
    q = apply_axial_rope(q.reshape(b, n_lat, ATT_HEADS, ATT_HEAD_DIM), cos, sin) * scale
    k = apply_axial_rope(k.reshape(b, n_lat, ATT_KV_HEADS, ATT_HEAD_DIM), cos, sin)
    v = v.reshape(b, n_lat, ATT_KV_HEADS, ATT_HEAD_DIM)
    if need_ctx:
        q_c, k_c, v_c = jnp.split(a_ctx @ w_in, (ATT_Q_W, ATT_Q_W + ATT_KV_W), axis=-1)
    else:
        k_c, v_c = jnp.split(a_ctx @ w_in[:, ATT_Q_W:], (ATT_KV_W,), axis=-1)
    k_c = k_c.reshape(b, n_ctx, ATT_KV_HEADS, ATT_HEAD_DIM)
    v_c = v_c.reshape(b, n_ctx, ATT_KV_HEADS, ATT_HEAD_DIM)
    sink = sinks.astype(jnp.float32).reshape(ATT_KV_HEADS, ATT_GROUP)

    span = ATT_BLOCK + 2 * WINDOW
    k_pad = jnp.pad(k, ((0, 0), (WINDOW, WINDOW), (0, 0), (0, 0)))
    v_pad = jnp.pad(v, ((0, 0), (WINDOW, WINDOW), (0, 0), (0, 0)))
    p_idx = jnp.arange(ATT_BLOCK)
    r_idx = jnp.arange(span)
    band = (r_idx[None, :] >= p_idx[:, None]) & (r_idx[None, :] <= p_idx[:, None] + 2 * WINDOW)

    def attend_block(blk):
        start = blk * ATT_BLOCK
        qb = lax.dynamic_slice_in_dim(q, start, ATT_BLOCK, axis=1).reshape(
            b, ATT_BLOCK, ATT_KV_HEADS, ATT_GROUP, ATT_HEAD_DIM)
        kb = lax.dynamic_slice_in_dim(k_pad, start, span, axis=1)
        vb = lax.dynamic_slice_in_dim(v_pad, start, span, axis=1)
        key_pos = start - WINDOW + r_idx
        valid = band & ((key_pos >= 0) & (key_pos < n_lat))[None, :]
        s_win = jnp.where(valid, jnp.einsum('bqkgd,bskd->bkgqs', qb, kb).astype(jnp.float32), NEG_INF)
        s_ctx = jnp.einsum('bqkgd,bskd->bkgqs', qb, k_c).astype(jnp.float32)
        s_sink = jnp.broadcast_to(sink[None, :, :, None, None], (b, ATT_KV_HEADS, ATT_GROUP, ATT_BLOCK, 1))
        probs = jax.nn.softmax(jnp.concatenate([s_win, s_ctx, s_sink], axis=-1), axis=-1).astype(v.dtype)
        o = (jnp.einsum('bkgqs,bskd->bqkgd', probs[..., :span], vb)
             + jnp.einsum('bkgqs,bskd->bqkgd', probs[..., span:span + n_ctx], v_c))
        return o.reshape(b, ATT_BLOCK, ATT_Q_W)

    o_lat = lax.map(attend_block, jnp.arange(n_lat // ATT_BLOCK))
    out_lat = jnp.moveaxis(o_lat, 0, 1).reshape(b, n_lat, ATT_Q_W) @ w_out
    if need_ctx:
        qc = q_c.reshape(b, n_ctx, ATT_KV_HEADS, ATT_GROUP, ATT_HEAD_DIM) * scale
        s = jnp.einsum('bqkgd,bskd->bkgqs', qc, k_c).astype(jnp.float32)
        s_sink = jnp.broadcast_to(sink[None, :, :, None, None], (b, ATT_KV_HEADS, ATT_GROUP, n_ctx, 1))
        probs = jax.nn.softmax(jnp.concatenate([s, s_sink], axis=-1), axis=-1).astype(v_c.dtype)
        o_c = jnp.einsum('bkgqs,bskd->bqkgd', probs[..., :n_ctx], v_c)
        out_ctx = o_c.reshape(b, n_ctx, ATT_Q_W) @ w_out
    else:
        out_ctx = None
    return out_lat, out_ctx


def peer_ffn(xn, wq, keys1, keys2, u_tab, v_tab):
    t, d = xn.shape
    half = PEER_D_KEY // 2
    q = (xn @ wq).reshape(t, PEER_HEADS, PEER_D_KEY)
    s1 = jnp.einsum('thd,hkd->thk', q[..., :half], keys1).astype(jnp.float32)
    s2 = jnp.einsum('thd,hkd->thk', q[..., half:], keys2).astype(jnp.float32)
    v1, i1 = lax.top_k(s1, PEER_TOPK)
    v2, i2 = lax.top_k(s2, PEER_TOPK)
    cand = (v1[..., :, None] + v2[..., None, :]).reshape(t, PEER_HEADS, PEER_TOPK * PEER_TOPK)
    cs, ci = lax.top_k(cand, PEER_TOPK)
    e1 = jnp.take_along_axis(i1, ci // PEER_TOPK, axis=-1)
    e2 = jnp.take_along_axis(i2, ci % PEER_TOPK, axis=-1)
    expert = e1 * PEER_N_KEYS + e2
    gate = jax.nn.softmax(cs, axis=-1).astype(xn.dtype)
    nb = t // PEER_TOKEN_BLOCK

    def block(args):
        xb, eb, gb = args
        ub = u_tab[eb]
        vb = v_tab[eb]
        act = jax.nn.gelu(jnp.einsum('td,thkd->thk', xb, ub), approximate=False)
        return jnp.einsum('thk,thkd->td', gb * act, vb)

    out = lax.map(block, (xn.reshape(nb, PEER_TOKEN_BLOCK, d),
                          expert.reshape(nb, PEER_TOKEN_BLOCK, PEER_HEADS, PEER_TOPK),
                          gate.reshape(nb, PEER_TOKEN_BLOCK, PEER_HEADS, PEER_TOPK)))
    return out.reshape(t, d)


def setup_inputs(seed: int = 0) -> dict:
    key = jax.random.key(seed)
    ks = jax.random.split(key, 24)
    f32 = jnp.float32
    nrm = lambda k, shape, s: jax.random.normal(k, shape, f32) * s
    ret_base = jnp.log(2.0 ** (5.0 + jnp.arange(RET_HEADS, dtype=f32)) - 1.0)
    return {
        'x': nrm(ks[0], (BATCH, SEQ, D_MODEL), 1.0),
        'c': nrm(ks[1], (BATCH, D_MODEL), 1.0),
        'ctx': nrm(ks[2], (BATCH, CTX_LEN, D_MODEL), 1.0),
        'c_ctx': nrm(ks[3], (D_MODEL,), 1.0),
        'ada_w': nrm(ks[4], (DEPTH, D_MODEL, N_ADA * D_MODEL), 0.5 * D_MODEL ** -0.5),
        'ada_b': nrm(ks[5], (DEPTH, N_ADA * D_MODEL), 0.01),
        'mix_norm_g': 1.0 + nrm(ks[6], (DEPTH, D_MODEL), 0.02),
        'ffn_norm_g': 1.0 + nrm(ks[7], (DEPTH, D_MODEL), 0.02),
        'ev_w_in': nrm(ks[8], (N_EVEN, D_MODEL, EV_IN_COLS), D_MODEL ** -0.5),
        'ev_w_out': nrm(ks[9], (N_EVEN, MIX_WIDTH, D_MODEL), MIX_WIDTH ** -0.5),
        'ret_decay_logit_f': ret_base[None] + nrm(ks[10], (N_EVEN, RET_HEADS), 0.1),
        'ret_decay_logit_b': ret_base[None] + nrm(ks[11], (N_EVEN, RET_HEADS), 0.1),
        'conv_w': nrm(ks[12], (N_EVEN, CONV_K, CONV_CH), CONV_K ** -0.5),
        'od_w_in': nrm(ks[13], (N_ODD, D_MODEL, OD_IN_COLS), D_MODEL ** -0.5),
        'od_w_out': nrm(ks[14], (N_ODD, MIX_WIDTH, D_MODEL), MIX_WIDTH ** -0.5),
        'attn_sinks': nrm(ks[15], (N_ODD, ATT_HEADS), 0.5),
        'peer_wq': nrm(ks[16], (DEPTH, D_MODEL, PEER_HEADS * PEER_D_KEY), D_MODEL ** -0.5),
        'peer_keys1': nrm(ks[17], (DEPTH, PEER_HEADS, PEER_N_KEYS, PEER_D_KEY // 2), (PEER_D_KEY // 2) ** -0.5),
        'peer_keys2': nrm(ks[18], (DEPTH, PEER_HEADS, PEER_N_KEYS, PEER_D_KEY // 2), (PEER_D_KEY // 2) ** -0.5),
        'peer_u': nrm(ks[19], (DEPTH, PEER_N_EXPERTS, D_MODEL), D_MODEL ** -0.5),
        'peer_v': nrm(ks[20], (DEPTH, PEER_N_EXPERTS, D_MODEL), PEER_HEADS ** -0.5),
        'final_norm_g': 1.0 + nrm(ks[21], (D_MODEL,), 0.02),
    }


def reference(x, c, ctx, c_ctx, ada_w, ada_b, mix_norm_g, ffn_norm_g, ev_w_in, ev_w_out,
              ret_decay_logit_f, ret_decay_logit_b, conv_w, od_w_in, od_w_out, attn_sinks,
              peer_wq, peer_keys1, peer_keys2, peer_u, peer_v, final_norm_g):
    h_lat, h_ctx = x, ctx
    silu_c = jax.nn.silu(c)
    silu_cc = jax.nn.silu(c_ctx)
    for layer in range(DEPTH):
        need_ctx = layer < DEPTH - 1
        j = layer // 2
        mod_lat = (silu_c @ ada_w[layer] + ada_b[layer])[:, None, :]
        mod_ctx = silu_cc @ ada_w[layer] + ada_b[layer]
        sh1, sc1, g1, sh2, sc2, g2 = jnp.split(mod_lat, N_ADA, axis=-1)
        csh1, csc1, cg1, csh2, csc2, cg2 = jnp.split(mod_ctx, N_ADA, axis=-1)
        a_lat = modulate(rmsnorm(h_lat, mix_norm_g[layer]), sh1, sc1)
        a_ctx = modulate(rmsnorm(h_ctx, mix_norm_g[layer]), csh1, csc1)
        if layer % 2 == 0:
            m_lat, m_ctx = even_mixer(a_lat, a_ctx, ev_w_in[j], ev_w_out[j], ret_decay_logit_f[j],
                                      ret_decay_logit_b[j], conv_w[j], need_ctx)
        else:
            m_lat, m_ctx = odd_mixer(a_lat, a_ctx, od_w_in[j], od_w_out[j], attn_sinks[j], need_ctx)
        h_lat = h_lat + g1 * m_lat
        f_lat = modulate(rmsnorm(h_lat, ffn_norm_g[layer]), sh2, sc2)
        if need_ctx:
            h_ctx = h_ctx + cg1 * m_ctx
            f_ctx = modulate(rmsnorm(h_ctx, ffn_norm_g[layer]), csh2, csc2)
            n_ctx_tok = f_ctx.shape[0] * f_ctx.shape[1]
            tokens = jnp.concatenate([f_ctx.reshape(-1, D_MODEL), f_lat.reshape(-1, D_MODEL)], axis=0)
            y = peer_ffn(tokens, peer_wq[layer], peer_keys1[layer], peer_keys2[layer], peer_u[layer], peer_v[layer])
            h_ctx = h_ctx + cg2 * y[:n_ctx_tok].reshape(h_ctx.shape)
            y_lat = y[n_ctx_tok:].reshape(h_lat.shape)
        else:
            y_lat = peer_ffn(f_lat.reshape(-1, D_MODEL), peer_wq[layer], peer_keys1[layer], peer_keys2[layer],
                             peer_u[layer], peer_v[layer]).reshape(h_lat.shape)
        h_lat = h_lat + g2 * y_lat
    return rmsnorm(h_lat, final_norm_g)
```

```python
import jax
import jax.numpy as jnp
from jax import lax
from jax.experimental import pallas as pl

D_MODEL = 1024
BATCH = 32
SEQ = 2048
DEPTH = 2

GRID_W = 64
CTX_LEN = 256
EPS = 1e-6
ROPE_BASE = 10000.0
NEG_INF = -1e30
N_ADA = 6
MIX_WIDTH = D_MODEL
RET_HEADS = 4
RET_V_DIM = MIX_WIDTH // (2 * RET_HEADS)
RET_QK_DIM = RET_V_DIM // 2
RET_CHUNK = 128
CONV_CH = MIX_WIDTH // 2
CONV_K = 3
ATT_HEADS = 16
ATT_HEAD_DIM = MIX_WIDTH // ATT_HEADS
ATT_KV_HEADS = 4
ATT_GROUP = ATT_HEADS // ATT_KV_HEADS
WINDOW = 128
ATT_BLOCK = 128
PEER_HEADS = 8
PEER_N_KEYS = 128
PEER_N_EXPERTS = PEER_N_KEYS * PEER_N_KEYS
PEER_D_KEY = 128
PEER_TOPK = 16
PEER_TOKEN_BLOCK = 128
N_EVEN = (DEPTH + 1) // 2
N_ODD = DEPTH // 2
RET_Q_W = RET_HEADS * RET_QK_DIM
RET_V_W = RET_HEADS * RET_V_DIM
EV_SPLITS = (RET_Q_W, 2 * RET_Q_W, 2 * RET_Q_W + RET_V_W, 2 * RET_Q_W + 2 * RET_V_W,
             2 * RET_Q_W + 2 * RET_V_W + CONV_CH, 2 * RET_Q_W + 2 * RET_V_W + 2 * CONV_CH)
EV_IN_COLS = 2 * RET_Q_W + 2 * RET_V_W + 3 * CONV_CH
ATT_Q_W = ATT_HEADS * ATT_HEAD_DIM
ATT_KV_W = ATT_KV_HEADS * ATT_HEAD_DIM
OD_IN_COLS = ATT_Q_W + 2 * ATT_KV_W


def rmsnorm(x, gain):
    x32 = x.astype(jnp.float32)
    y = x32 * lax.rsqrt(jnp.mean(x32 * x32, axis=-1, keepdims=True) + EPS)
    return (y * gain.astype(jnp.float32)).astype(x.dtype)


def head_rmsnorm(y):
    y32 = y.astype(jnp.float32)
    return y32 * lax.rsqrt(jnp.mean(y32 * y32, axis=-1, keepdims=True) + EPS)


def modulate(x, shift, scale):
    return x * (1.0 + scale) + shift


def axial_rope_tables(n_tok, head_dim):
    n_rows = n_tok // GRID_W
    rows = jnp.broadcast_to(jnp.arange(n_rows, dtype=jnp.float32)[:, None], (n_rows, GRID_W)).reshape(-1)
    cols = jnp.broadcast_to(jnp.arange(GRID_W, dtype=jnp.float32)[None, :], (n_rows, GRID_W)).reshape(-1)
    n_freq = head_dim // 4
    inv_freq = ROPE_BASE ** (-jnp.arange(n_freq, dtype=jnp.float32) / n_freq)
    ang = jnp.concatenate([rows[:, None] * inv_freq, cols[:, None] * inv_freq], axis=-1)
    return jnp.cos(ang), jnp.sin(ang)


def _rotate_half(u, cos, sin):
    n = u.shape[-1] // 2
    u1, u2 = u[..., :n], u[..., n:]
    return jnp.concatenate([u1 * cos - u2 * sin, u2 * cos + u1 * sin], axis=-1)


def apply_axial_rope(x, cos, sin):
    n = x.shape[-1] // 4
    c = cos[None, :, None, :]
    s = sin[None, :, None, :]
    xr = _rotate_half(x[..., :2 * n], c[..., :n], s[..., :n])
    xc = _rotate_half(x[..., 2 * n:], c[..., n:], s[..., n:])
    return jnp.concatenate([xr, xc], axis=-1).astype(x.dtype)


def short_conv(u, w):
    t = u.shape[1]
    pad = CONV_K // 2
    up = jnp.pad(u, ((0, 0), (pad, pad), (0, 0)))
    return sum(w[i] * up[:, i:i + t] for i in range(CONV_K))


def retention_chunked(q, k, v, log_gamma, state0):
    b, t, h, dk = q.shape
    dv = v.shape[-1]
    n = t // RET_CHUNK
    qc = q.reshape(b, n, RET_CHUNK, h, dk)
    kc = k.reshape(b, n, RET_CHUNK, h, dk)
    vc = v.reshape(b, n, RET_CHUNK, h, dv)
    pos = jnp.arange(RET_CHUNK, dtype=jnp.float32)
    diff = pos[:, None] - pos[None, :]
    decay_in = jnp.where(diff[None] >= 0,
                         jnp.exp(log_gamma[:, None, None] * jnp.maximum(diff, 0.0)[None]), 0.0)
    scores = jnp.einsum('bnihd,bnjhd->bnhij', qc, kc) * decay_in
    inner = jnp.einsum('bnhij,bnjhe->bnihe', scores, vc)
    zeta = jnp.exp(log_gamma[:, None] * (RET_CHUNK - 1 - pos)[None])
    kv = jnp.einsum('bnjhd,bnjhe,hj->nbhde', kc, vc, zeta)
    chunk_decay = jnp.exp(log_gamma * RET_CHUNK)[None, :, None, None]

    def step(state, kv_i):
        return chunk_decay * state + kv_i, state

    state_final, state_prev = lax.scan(step, state0, kv)
    xi = jnp.exp(log_gamma[None, :] * (pos[:, None] + 1.0))
    cross = jnp.einsum('bnihd,nbhde->bnihe', qc, state_prev) * xi[None, None, :, :, None]
    return (inner + cross).reshape(b, t, h, dv), state_final


def retention_final_state(k, v, log_gamma):
    t = k.shape[1]
    pos = jnp.arange(t, dtype=jnp.float32)
    w = jnp.exp(log_gamma[:, None] * (t - 1 - pos)[None])
    return jnp.einsum('bthd,bthe,ht->bhde', k, v, w)


def bidirectional_retention(q_l, k_l, v_l, q_c, k_c, v_c, log_f, log_b):
    b, _, h, dk = k_c.shape
    dv = v_c.shape[-1]
    flip = lambda a: a[:, ::-1]
    zeros = jnp.zeros((b, h, dk, dv), jnp.float32)
    if q_c is not None:
        ctx_f, s_f = retention_chunked(q_c, k_c, v_c, log_f, zeros)
        ctx_b, s_b = retention_chunked(flip(q_c), flip(k_c), flip(v_c), log_b, zeros)
        out_ctx = ctx_f + flip(ctx_b)
    else:
        s_f = retention_final_state(k_c, v_c, log_f)
        s_b = retention_final_state(flip(k_c), flip(v_c), log_b)
        out_ctx = None
    lat_f, _ = retention_chunked(q_l, k_l, v_l, log_f, s_f)
    lat_b, _ = retention_chunked(flip(q_l), flip(k_l), flip(v_l), log_b, s_b)
    return lat_f + flip(lat_b), out_ctx


def even_output(ret, g, gb, gc, xt, conv_w, w_out):
    b, t = ret.shape[:2]
    y_ret = (head_rmsnorm(ret).reshape(b, t, RET_V_W) * jax.nn.silu(g.astype(jnp.float32))).astype(g.dtype)
    y_conv = gb * short_conv(gc * xt, conv_w)
    return jnp.concatenate([y_ret, y_conv], axis=-1) @ w_out


def even_mixer(a_lat, a_ctx, w_in, w_out, logit_f, logit_b, conv_w, need_ctx):
    b, n_lat, _ = a_lat.shape
    n_ctx = a_ctx.shape[1]
    k_scale = RET_QK_DIM ** -0.5
    log_f = jax.nn.log_sigmoid(logit_f.astype(jnp.float32))
    log_b = jax.nn.log_sigmoid(logit_b.astype(jnp.float32))
    cos, sin = axial_rope_tables(n_lat, RET_QK_DIM)
    q_l, k_l, v_l, g_l, gb_l, gc_l, x_l = jnp.split(a_lat @ w_in, EV_SPLITS, axis=-1)
    q_l = apply_axial_rope(q_l.reshape(b, n_lat, RET_HEADS, RET_QK_DIM), cos, sin)
    k_l = apply_axial_rope(k_l.reshape(b, n_lat, RET_HEADS, RET_QK_DIM), cos, sin) * k_scale
    v_l = v_l.reshape(b, n_lat, RET_HEADS, RET_V_DIM)
    if need_ctx:
        q_c, k_c, v_c, g_c, gb_c, gc_c, x_c = jnp.split(a_ctx @ w_in, EV_SPLITS, axis=-1)
        q_c = q_c.reshape(b, n_ctx, RET_HEADS, RET_QK_DIM)
    else:
        k_c, v_c = jnp.split(a_ctx @ w_in[:, RET_Q_W:2 * RET_Q_W + RET_V_W], (RET_Q_W,), axis=-1)
        q_c = None
    k_c = k_c.reshape(b, n_ctx, RET_HEADS, RET_QK_DIM) * k_scale
    v_c = v_c.reshape(b, n_ctx, RET_HEADS, RET_V_DIM)
    ret_l, ret_c = bidirectional_retention(q_l, k_l, v_l, q_c, k_c, v_c, log_f, log_b)
    out_lat = even_output(ret_l, g_l, gb_l, gc_l, x_l, conv_w, w_out)
    out_ctx = even_output(ret_c, g_c, gb_c, gc_c, x_c, conv_w, w_out) if need_ctx else None
    return out_lat, out_ctx


def odd_mixer(a_lat, a_ctx, w_in, w_out, sinks, need_ctx):
    b, n_lat, _ = a_lat.shape
    n_ctx = a_ctx.shape[1]
    scale = ATT_HEAD_DIM ** -0.5
    cos, sin = axial_rope_tables(n_lat, ATT_HEAD_DIM)
    q, k, v = jnp.split(a_lat @ w_in, (ATT_Q_W, ATT_Q_W + ATT_KV_W), axis=-1)
    q = apply_axial_rope(q.reshape(b, n_lat, ATT_HEADS, ATT_HEAD_DIM), cos, sin) * scale
    k = apply_axial_rope(k.reshape(b, n_lat, ATT_KV_HEADS, ATT_HEAD_DIM), cos, sin)
    v = v.reshape(b, n_lat, ATT_KV_HEADS, ATT_HEAD_DIM)
    if need_ctx:
        q_c, k_c, v_c = jnp.split(a_ctx @ w_in, (ATT_Q_W, ATT_Q_W + ATT_KV_W), axis=-1)
    else:
        k_c, v_c = jnp.split(a_ctx @ w_in[:, ATT_Q_W:], (ATT_KV_W,), axis=-1)
    k_c = k_c.reshape(b, n_ctx, ATT_KV_HEADS, ATT_HEAD_DIM)
    v_c = v_c.reshape(b, n_ctx, ATT_KV_HEADS, ATT_HEAD_DIM)
    sink = sinks.astype(jnp.float32).reshape(ATT_KV_HEADS, ATT_GROUP)

    span = ATT_BLOCK + 2 * WINDOW
    k_pad = jnp.pad(k, ((0, 0), (WINDOW, WINDOW), (0, 0), (0, 0)))
    v_pad = jnp.pad(v, ((0, 0), (WINDOW, WINDOW), (0, 0), (0, 0)))
    p_idx = jnp.arange(ATT_BLOCK)
    r_idx = jnp.arange(span)
    band = (r_idx[None, :] >= p_idx[:, None]) & (r_idx[None, :] <= p_idx[:, None] + 2 * WINDOW)

    def attend_block(blk):
        start = blk * ATT_BLOCK
        qb = lax.dynamic_slice_in_dim(q, start, ATT_BLOCK, axis=1).reshape(
            b, ATT_BLOCK, ATT_KV_HEADS, ATT_GROUP, ATT_HEAD_DIM)
        kb = lax.dynamic_slice_in_dim(k_pad, start, span, axis=1)
        vb = lax.dynamic_slice_in_dim(v_pad, start, span, axis=1)
        key_pos = start - WINDOW + r_idx
        valid = band & ((key_pos >= 0) & (key_pos < n_lat))[None, :]
        s_win = jnp.where(valid, jnp.einsum('bqkgd,bskd->bkgqs', qb, kb).astype(jnp.float32), NEG_INF)
        s_ctx = jnp.einsum('bqkgd,bskd->bkgqs', qb, k_c).astype(jnp.float32)
        s_sink = jnp.broadcast_to(sink[None, :, :, None, None], (b, ATT_KV_HEADS, ATT_GROUP, ATT_BLOCK, 1))
        probs = jax.nn.softmax(jnp.concatenate([s_win, s_ctx, s_sink], axis=-1), axis=-1).astype(v.dtype)
        o = (jnp.einsum('bkgqs,bskd->bqkgd', probs[..., :span], vb)
             + jnp.einsum('bkgqs,bskd->bqkgd', probs[..., span:span + n_ctx], v_c))
        return o.reshape(b, ATT_BLOCK, ATT_Q_W)

    o_lat = lax.map(attend_block, jnp.arange(n_lat // ATT_BLOCK))
    out_lat = jnp.moveaxis(o_lat, 0, 1).reshape(b, n_lat, ATT_Q_W) @ w_out
    if need_ctx:
        qc = q_c.reshape(b, n_ctx, ATT_KV_HEADS, ATT_GROUP, ATT_HEAD_DIM) * scale
        s = jnp.einsum('bqkgd,bskd->bkgqs', qc, k_c).astype(jnp.float32)
        s_sink = jnp.broadcast_to(sink[None, :, :, None, None], (b, ATT_KV_HEADS, ATT_GROUP, n_ctx, 1))
        probs = jax.nn.softmax(jnp.concatenate([s, s_sink], axis=-1), axis=-1).astype(v_c.dtype)
        o_c = jnp.einsum('bkgqs,bskd->bqkgd', probs[..., :n_ctx], v_c)
        out_ctx = o_c.reshape(b, n_ctx, ATT_Q_W) @ w_out
    else:
        out_ctx = None
    return out_lat, out_ctx


def peer_ffn(xn, wq, keys1, keys2, u_tab, v_tab):
    t, d = xn.shape
    half = PEER_D_KEY // 2
    q = (xn @ wq).reshape(t, PEER_HEADS, PEER_D_KEY)
    s1 = jnp.einsum('thd,hkd->thk', q[..., :half], keys1).astype(jnp.float32)
    s2 = jnp.einsum('thd,hkd->thk', q[..., half:], keys2).astype(jnp.float32)
    v1, i1 = lax.top_k(s1, PEER_TOPK)
    v2, i2 = lax.top_k(s2, PEER_TOPK)
    cand = (v1[..., :, None] + v2[..., None, :]).reshape(t, PEER_HEADS, PEER_TOPK * PEER_TOPK)
    cs, ci = lax.top_k(cand, PEER_TOPK)
    e1 = jnp.take_along_axis(i1, ci // PEER_TOPK, axis=-1)
    e2 = jnp.take_along_axis(i2, ci % PEER_TOPK, axis=-1)
    expert = e1 * PEER_N_KEYS + e2
    gate = jax.nn.softmax(cs, axis=-1).astype(xn.dtype)
    nb = t // PEER_TOKEN_BLOCK

    def block(args):
        xb, eb, gb = args
        ub = u_tab[eb]
        vb = v_tab[eb]
        act = jax.nn.gelu(jnp.einsum('td,thkd->thk', xb, ub), approximate=False)
        return jnp.einsum('thk,thkd->td', gb * act, vb)

    out = lax.map(block, (xn.reshape(nb, PEER_TOKEN_BLOCK, d),
                          expert.reshape(nb, PEER_TOKEN_BLOCK, PEER_HEADS, PEER_TOPK),
                          gate.reshape(nb, PEER_TOKEN_BLOCK, PEER_HEADS, PEER_TOPK)))
    return out.reshape(t, d)


def _final_norm_kernel(x_ref, g_ref, o_ref):
    x = x_ref[...]
    y = x * lax.rsqrt(jnp.mean(x * x, axis=-1, keepdims=True) + EPS)
    o_ref[...] = y * g_ref[...]


def _final_rmsnorm(h, g):
    b, t, d = h.shape
    x = h.reshape(b * t, d)
    rows = 1024
    out = pl.pallas_call(
        _final_norm_kernel,
        grid=(b * t // rows,),
        in_specs=[pl.BlockSpec((rows, d), lambda i: (i, 0)), pl.BlockSpec((1, d), lambda i: (0, 0))],
        out_specs=pl.BlockSpec((rows, d), lambda i: (i, 0)),
        out_shape=jax.ShapeDtypeStruct((b * t, d), h.dtype),
    )(x, g.reshape(1, d))
    return out.reshape(b, t, d)


def kernel(x, c, ctx, c_ctx, ada_w, ada_b, mix_norm_g, ffn_norm_g, ev_w_in, ev_w_out,
           ret_decay_logit_f, ret_decay_logit_b, conv_w, od_w_in, od_w_out, attn_sinks,
           peer_wq, peer_keys1, peer_keys2, peer_u, peer_v, final_norm_g):
    h_lat, h_ctx = x, ctx
    silu_c = jax.nn.silu(c)
    silu_cc = jax.nn.silu(c_ctx)
    for layer in range(DEPTH):
        need_ctx = layer < DEPTH - 1
        j = layer // 2
        mod_lat = (silu_c @ ada_w[layer] + ada_b[layer])[:, None, :]
        mod_ctx = silu_cc @ ada_w[layer] + ada_b[layer]
        sh1, sc1, g1, sh2, sc2, g2 = jnp.split(mod_lat, N_ADA, axis=-1)
        csh1, csc1, cg1, csh2, csc2, cg2 = jnp.split(mod_ctx, N_ADA, axis=-1)
        a_lat = modulate(rmsnorm(h_lat, mix_norm_g[layer]), sh1, sc1)
        a_ctx = modulate(rmsnorm(h_ctx, mix_norm_g[layer]), csh1, csc1)
        if layer % 2 == 0:
            m_lat, m_ctx = even_mixer(a_lat, a_ctx, ev_w_in[j], ev_w_out[j], ret_decay_logit_f[j],
                                      ret_decay_logit_b[j], conv_w[j], need_ctx)
        else:
            m_lat, m_ctx = odd_mixer(a_lat, a_ctx, od_w_in[j], od_w_out[j], attn_sinks[j], need_ctx)
        h_lat = h_lat + g1 * m_lat
        f_lat = modulate(rmsnorm(h_lat, ffn_norm_g[layer]), sh2, sc2)
        if need_ctx:
            h_ctx = h_ctx + cg1 * m_ctx
            f_ctx = modulate(rmsnorm(h_ctx, ffn_norm_g[layer]), csh2, csc2)
            n_ctx_tok = f_ctx.shape[0] * f_ctx.shape[1]
            tokens = jnp.concatenate([f_ctx.reshape(-1, D_MODEL), f_lat.reshape(-1, D_MODEL)], axis=0)
            y = peer_ffn(tokens, peer_wq[layer], peer_keys1[layer], peer_keys2[layer], peer_u[layer], peer_v[layer])
            h_ctx = h_ctx + cg2 * y[:n_ctx_tok].reshape(h_ctx.shape)
            y_lat = y[n_ctx_tok:].reshape(h_lat.shape)
        else:
            y_lat = peer_ffn(f_lat.reshape(-1, D_MODEL), peer_wq[layer], peer_keys1[layer], peer_keys2[layer],
                             peer_u[layer], peer_v[layer]).reshape(h_lat.shape)
        h_lat = h_lat + g2 * y_lat
    return _final_rmsnorm(h_lat, final_norm_g)
```

```python
import jax
import jax.numpy as jnp
from jax import lax
from jax.experimental import pallas as pl
from jax.experimental.pallas import tpu as pltpu

D_MODEL = 1024
BATCH = 32
SEQ = 2048
DEPTH = 2

GRID_W = 64
CTX_LEN = 256
EPS = 1e-6
ROPE_BASE = 10000.0
NEG_INF = -1e30
N_ADA = 6
MIX_WIDTH = D_MODEL
RET_HEADS = 4
RET_V_DIM = MIX_WIDTH // (2 * RET_HEADS)
RET_QK_DIM = RET_V_DIM // 2
RET_CHUNK = 128
CONV_CH = MIX_WIDTH // 2
CONV_K = 3
ATT_HEADS = 16
ATT_HEAD_DIM = MIX_WIDTH // ATT_HEADS
ATT_KV_HEADS = 4
ATT_GROUP = ATT_HEADS // ATT_KV_HEADS
WINDOW = 128
ATT_BLOCK = 128
PEER_HEADS = 8
PEER_N_KEYS = 128
PEER_N_EXPERTS = PEER_N_KEYS * PEER_N_KEYS
PEER_D_KEY = 128
PEER_TOPK = 16
PEER_TOKEN_BLOCK = 128
N_EVEN = (DEPTH + 1) // 2
N_ODD = DEPTH // 2
RET_Q_W = RET_HEADS * RET_QK_DIM
RET_V_W = RET_HEADS * RET_V_DIM
EV_SPLITS = (RET_Q_W, 2 * RET_Q_W, 2 * RET_Q_W + RET_V_W, 2 * RET_Q_W + 2 * RET_V_W,
             2 * RET_Q_W + 2 * RET_V_W + CONV_CH, 2 * RET_Q_W + 2 * RET_V_W + 2 * CONV_CH)
EV_IN_COLS = 2 * RET_Q_W + 2 * RET_V_W + 3 * CONV_CH
ATT_Q_W = ATT_HEADS * ATT_HEAD_DIM
ATT_KV_W = ATT_KV_HEADS * ATT_HEAD_DIM
OD_IN_COLS = ATT_Q_W + 2 * ATT_KV_W


def rmsnorm(x, gain):
    x32 = x.astype(jnp.float32)
    y = x32 * lax.rsqrt(jnp.mean(x32 * x32, axis=-1, keepdims=True) + EPS)
    return (y * gain.astype(jnp.float32)).astype(x.dtype)


def head_rmsnorm(y):
    y32 = y.astype(jnp.float32)
    return y32 * lax.rsqrt(jnp.mean(y32 * y32, axis=-1, keepdims=True) + EPS)


def modulate(x, shift, scale):
    return x * (1.0 + scale) + shift


def axial_rope_tables(n_tok, head_dim):
    n_rows = n_tok // GRID_W
    rows = jnp.broadcast_to(jnp.arange(n_rows, dtype=jnp.float32)[:, None], (n_rows, GRID_W)).reshape(-1)
    cols = jnp.broadcast_to(jnp.arange(GRID_W, dtype=jnp.float32)[None, :], (n_rows, GRID_W)).reshape(-1)
    n_freq = head_dim // 4
    inv_freq = ROPE_BASE ** (-jnp.arange(n_freq, dtype=jnp.float32) / n_freq)
    ang = jnp.concatenate([rows[:, None] * inv_freq, cols[:, None] * inv_freq], axis=-1)
    return jnp.cos(ang), jnp.sin(ang)


def _rotate_half(u, cos, sin):
    n = u.shape[-1] // 2
    u1, u2 = u[..., :n], u[..., n:]
    return jnp.concatenate([u1 * cos - u2 * sin, u2 * cos + u1 * sin], axis=-1)


def apply_axial_rope(x, cos, sin):
    n = x.shape[-1] // 4
    c = cos[None, :, None, :]
    s = sin[None, :, None, :]
    xr = _rotate_half(x[..., :2 * n], c[..., :n], s[..., :n])
    xc = _rotate_half(x[..., 2 * n:], c[..., n:], s[..., n:])
    return jnp.concatenate([xr, xc], axis=-1).astype(x.dtype)


def short_conv(u, w):
    t = u.shape[1]
    pad = CONV_K // 2
    up = jnp.pad(u, ((0, 0), (pad, pad), (0, 0)))
    return sum(w[i] * up[:, i:i + t] for i in range(CONV_K))


def retention_chunked(q, k, v, log_gamma, state0):
    b, t, h, dk = q.shape
    dv = v.shape[-1]
    n = t // RET_CHUNK
    qc = q.reshape(b, n, RET_CHUNK, h, dk)
    kc = k.reshape(b, n, RET_CHUNK, h, dk)
    vc = v.reshape(b, n, RET_CHUNK, h, dv)
    pos = jnp.arange(RET_CHUNK, dtype=jnp.float32)
    diff = pos[:, None] - pos[None, :]
    decay_in = jnp.where(diff[None] >= 0,
                         jnp.exp(log_gamma[:, None, None] * jnp.maximum(diff, 0.0)[None]), 0.0)
    scores = jnp.einsum('bnihd,bnjhd->bnhij', qc, kc) * decay_in
    inner = jnp.einsum('bnhij,bnjhe->bnihe', scores, vc)
    zeta = jnp.exp(log_gamma[:, None] * (RET_CHUNK - 1 - pos)[None])
    kv = jnp.einsum('bnjhd,bnjhe,hj->nbhde', kc, vc, zeta)
    chunk_decay = jnp.exp(log_gamma * RET_CHUNK)[None, :, None, None]

    def step(state, kv_i):
        return chunk_decay * state + kv_i, state

    state_final, state_prev = lax.scan(step, state0, kv)
    xi = jnp.exp(log_gamma[None, :] * (pos[:, None] + 1.0))
    cross = jnp.einsum('bnihd,nbhde->bnihe', qc, state_prev) * xi[None, None, :, :, None]
    return (inner + cross).reshape(b, t, h, dv), state_final


def retention_final_state(k, v, log_gamma):
    t = k.shape[1]
    pos = jnp.arange(t, dtype=jnp.float32)
    w = jnp.exp(log_gamma[:, None] * (t - 1 - pos)[None])
    return jnp.einsum('bthd,bthe,ht->bhde', k, v, w)


def bidirectional_retention(q_l, k_l, v_l, q_c, k_c, v_c, log_f, log_b):
    b, _, h, dk = k_c.shape
    dv = v_c.shape[-1]
    flip = lambda a: a[:, ::-1]
    zeros = jnp.zeros((b, h, dk, dv), jnp.float32)
    if q_c is not None:
        ctx_f, s_f = retention_chunked(q_c, k_c, v_c, log_f, zeros)
        ctx_b, s_b = retention_chunked(flip(q_c), flip(k_c), flip(v_c), log_b, zeros)
        out_ctx = ctx_f + flip(ctx_b)
    else:
        s_f = retention_final_state(k_c, v_c, log_f)
        s_b = retention_final_state(flip(k_c), flip(v_c), log_b)
        out_ctx = None
    lat_f, _ = retention_chunked(q_l, k_l, v_l, log_f, s_f)
    lat_b, _ = retention_chunked(flip(q_l), flip(k_l), flip(v_l), log_b, s_b)
    return lat_f + flip(lat_b), out_ctx


def even_output(ret, g, gb, gc, xt, conv_w, w_out):
    b, t = ret.shape[:2]
    y_ret = (head_rmsnorm(ret).reshape(b, t, RET_V_W) * jax.nn.silu(g.astype(jnp.float32))).astype(g.dtype)
    y_conv = gb * short_conv(gc * xt, conv_w)
    return jnp.concatenate([y_ret, y_conv], axis=-1) @ w_out


def even_mixer(a_lat, a_ctx, w_in, w_out, logit_f, logit_b, conv_w, need_ctx):
    b, n_lat, _ = a_lat.shape
    n_ctx = a_ctx.shape[1]
    k_scale = RET_QK_DIM ** -0.5
    log_f = jax.nn.log_sigmoid(logit_f.astype(jnp.float32))
    log_b = jax.nn.log_sigmoid(logit_b.astype(jnp.float32))
    cos, sin = axial_rope_tables(n_lat, RET_QK_DIM)
    q_l, k_l, v_l, g_l, gb_l, gc_l, x_l = jnp.split(a_lat @ w_in, EV_SPLITS, axis=-1)
    q_l = apply_axial_rope(q_l.reshape(b, n_lat, RET_HEADS, RET_QK_DIM), cos, sin)
    k_l = apply_axial_rope(k_l.reshape(b, n_lat, RET_HEADS, RET_QK_DIM), cos, sin) * k_scale
    v_l = v_l.reshape(b, n_lat, RET_HEADS, RET_V_DIM)
    if need_ctx:
        q_c, k_c, v_c, g_c, gb_c, gc_c, x_c = jnp.split(a_ctx @ w_in, EV_SPLITS, axis=-1)
        q_c = q_c.reshape(b, n_ctx, RET_HEADS, RET_QK_DIM)
    else:
        k_c, v_c = jnp.split(a_ctx @ w_in[:, RET_Q_W:2 * RET_Q_W + RET_V_W], (RET_Q_W,), axis=-1)
        q_c = None
    k_c = k_c.reshape(b, n_ctx, RET_HEADS, RET_QK_DIM) * k_scale
    v_c = v_c.reshape(b, n_ctx, RET_HEADS, RET_V_DIM)
    ret_l, ret_c = bidirectional_retention(q_l, k_l, v_l, q_c, k_c, v_c, log_f, log_b)
    out_lat = even_output(ret_l, g_l, gb_l, gc_l, x_l, conv_w, w_out)
    out_ctx = even_output(ret_c, g_c, gb_c, gc_c, x_c, conv_w, w_out) if need_ctx else None
    return out_lat, out_ctx


def odd_mixer(a_lat, a_ctx, w_in, w_out, sinks, need_ctx):
    b, n_lat, _ = a_lat.shape
    n_ctx = a_ctx.shape[1]
    scale = ATT_HEAD_DIM ** -0.5
    cos, sin = axial_rope_tables(n_lat, ATT_HEAD_DIM)
    q, k, v = jnp.split(a_lat @ w_in, (ATT_Q_W, ATT_Q_W + ATT_KV_W), axis=-1)
    q = apply_axial_rope(q.reshape(b, n_lat, ATT_HEADS, ATT_HEAD_DIM), cos, sin) * scale
    k = apply_axial_rope(k.reshape(b, n_lat, ATT_KV_HEADS, ATT_HEAD_DIM), cos, sin)
    v = v.reshape(b, n_lat, ATT_KV_HEADS, ATT_HEAD_DIM)
    if need_ctx:
        q_c, k_c, v_c = jnp.split(a_ctx @ w_in, (ATT_Q_W, ATT_Q_W + ATT_KV_W), axis=-1)
    else:
        k_c, v_c = jnp.split(a_ctx @ w_in[:, ATT_Q_W:], (ATT_KV_W,), axis=-1)
    k_c = k_c.reshape(b, n_ctx, ATT_KV_HEADS, ATT_HEAD_DIM)
    v_c = v_c.reshape(b, n_ctx, ATT_KV_HEADS, ATT_HEAD_DIM)
    sink = sinks.astype(jnp.float32).reshape(ATT_KV_HEADS, ATT_GROUP)

    span = ATT_BLOCK + 2 * WINDOW
    k_pad = jnp.pad(k, ((0, 0), (WINDOW, WINDOW), (0, 0), (0, 0)))
    v_pad = jnp.pad(v, ((0, 0), (WINDOW, WINDOW), (0, 0), (0, 0)))
    p_idx = jnp.arange(ATT_BLOCK)
    r_idx = jnp.arange(span)
    band = (r_idx[None, :] >= p_idx[:, None]) & (r_idx[None, :] <= p_idx[:, None] + 2 * WINDOW)

    def attend_block(blk):
        start = blk * ATT_BLOCK
        qb = lax.dynamic_slice_in_dim(q, start, ATT_BLOCK, axis=1).reshape(
            b, ATT_BLOCK, ATT_KV_HEADS, ATT_GROUP, ATT_HEAD_DIM)
        kb = lax.dynamic_slice_in_dim(k_pad, start, span, axis=1)
        vb = lax.dynamic_slice_in_dim(v_pad, start, span, axis=1)
        key_pos = start - WINDOW + r_idx
        valid = band & ((key_pos >= 0) & (key_pos < n_lat))[None, :]
        s_win = jnp.where(valid, jnp.einsum('bqkgd,bskd->bkgqs', qb, kb).astype(jnp.float32), NEG_INF)
        s_ctx = jnp.einsum('bqkgd,bskd->bkgqs', qb, k_c).astype(jnp.float32)
        s_sink = jnp.broadcast_to(sink[None, :, :, None, None], (b, ATT_KV_HEADS, ATT_GROUP, ATT_BLOCK, 1))
        probs = jax.nn.softmax(jnp.concatenate([s_win, s_ctx, s_sink], axis=-1), axis=-1).astype(v.dtype)
        o = (jnp.einsum('bkgqs,bskd->bqkgd', probs[..., :span], vb)
             + jnp.einsum('bkgqs,bskd->bqkgd', probs[..., span:span + n_ctx], v_c))
        return o.reshape(b, ATT_BLOCK, ATT_Q_W)

    o_lat = lax.map(attend_block, jnp.arange(n_lat // ATT_BLOCK))
    out_lat = jnp.moveaxis(o_lat, 0, 1).reshape(b, n_lat, ATT_Q_W) @ w_out
    if need_ctx:
        qc = q_c.reshape(b, n_ctx, ATT_KV_HEADS, ATT_GROUP, ATT_HEAD_DIM) * scale
        s = jnp.einsum('bqkgd,bskd->bkgqs', qc, k_c).astype(jnp.float32)
        s_sink = jnp.broadcast_to(sink[None, :, :, None, None], (b, ATT_KV_HEADS, ATT_GROUP, n_ctx, 1))
        probs = jax.nn.softmax(jnp.concatenate([s, s_sink], axis=-1), axis=-1).astype(v_c.dtype)
        o_c = jnp.einsum('bkgqs,bskd->bqkgd', probs[..., :n_ctx], v_c)
        out_ctx = o_c.reshape(b, n_ctx, ATT_Q_W) @ w_out
    else:
        out_ctx = None
    return out_lat, out_ctx


ROUTE_TOKENS = 128
PEER_TOKENS = 32
N_SLOTS = PEER_HEADS * PEER_TOPK
HALF_D = D_MODEL // 2
SLAB = HALF_D // 128
BF16_HI_MASK = -65536
VMEM_TABLE_LIMIT = 48 * 1024 * 1024
ID_BIG = 1e9
SQRT_HALF = 0.7071067811865476


def _pack_bf16_table(tab):
    e = tab.shape[0]
    bits = lax.bitcast_convert_type(tab.astype(jnp.bfloat16), jnp.uint16).astype(jnp.uint32)
    word = (bits[:, HALF_D:] << 16) | bits[:, :HALF_D]
    return lax.bitcast_convert_type(word, jnp.int32).reshape(e, SLAB, 128)


def _unpack_words(words):
    lo = lax.bitcast_convert_type(words << 16, jnp.float32)
    hi = lax.bitcast_convert_type(words & BF16_HI_MASK, jnp.float32)
    return lo, hi


def _extract_top(s, order, n):
    vals, ids = [], []
    for _ in range(n):
        m = jnp.max(s, axis=0, keepdims=True)
        first = jnp.min(jnp.where(s == m, order, ID_BIG), axis=0, keepdims=True)
        vals.append(m)
        ids.append(first)
        s = jnp.where(order == first, -jnp.inf, s)
    return vals, ids


def _route_kernel(f_ref, wq_ref, k1_ref, k2_ref, idx_ref, gate_ref,
                  v1_ref, v2_ref, i1_ref, i2_ref, et_ref, gt_ref):
    tb = f_ref.shape[0]
    q = jnp.dot(f_ref[...].astype(jnp.bfloat16), wq_ref[...], preferred_element_type=jnp.float32)
    qb = q.astype(jnp.bfloat16)
    key_id = lax.broadcasted_iota(jnp.int32, (PEER_N_KEYS, tb), 0).astype(jnp.float32)
    r8 = lax.broadcasted_iota(jnp.int32, (8, tb), 0).astype(jnp.float32)
    nt = (((1,), (1,)), ((), ()))
    flat = jnp.concatenate([r8 * 16, (r8 + 8) * 16, r8 * 16 + 1, r8 + 8, r8, r8 + 16, r8 + 32, r8 + 48, r8 + 64], axis=0)
    for h in range(PEER_HEADS):
        qh = qb[:, h * PEER_D_KEY:(h + 1) * PEER_D_KEY]
        s1 = lax.dot_general(k1_ref[h], qh, nt, preferred_element_type=jnp.float32)
        s2 = lax.dot_general(k2_ref[h], qh, nt, preferred_element_type=jnp.float32)
        for s, v_ref, i_ref in ((s1, v1_ref, i1_ref), (s2, v2_ref, i2_ref)):
            vals, ids = _extract_top(s, key_id, PEER_TOPK)
            for k in range(PEER_TOPK):
                v_ref[k:k + 1, :] = vals[k]
                i_ref[k:k + 1, :] = ids[k]
        v1a, v1b, v2a, v2b = v1_ref[0:8, :], v1_ref[8:16, :], v2_ref[0:8, :], v2_ref[8:16, :]
        i1a, i1b, i2a, i2b = i1_ref[0:8, :], i1_ref[8:16, :], i2_ref[0:8, :], i2_ref[8:16, :]
        ninf = -jnp.inf
        cand = jnp.concatenate([
            v1a + v2a[0:1], v1b + v2a[0:1], v1a + v2a[1:2], v2b + v1a[0:1],
            jnp.where(r8 >= 2, v2a + v1a[0:1], ninf),
            jnp.where(r8 >= 2, v2a + v1a[1:2], ninf),
            jnp.where((r8 >= 2) & (r8 <= 4), v2a + v1a[2:3], ninf),
            jnp.where((r8 >= 2) & (r8 <= 3), v2a + v1a[3:4], ninf),
            jnp.where(r8 == 2, v2a + v1a[4:5], ninf)], axis=0)
        expert = jnp.concatenate([
            i1a * PEER_N_KEYS + i2a[0:1], i1b * PEER_N_KEYS + i2a[0:1], i1a * PEER_N_KEYS + i2a[1:2],
            i1a[0:1] * PEER_N_KEYS + i2b,
            i1a[0:1] * PEER_N_KEYS + i2a, i1a[1:2] * PEER_N_KEYS + i2a, i1a[2:3] * PEER_N_KEYS + i2a,
            i1a[3:4] * PEER_N_KEYS + i2a, i1a[4:5] * PEER_N_KEYS + i2a], axis=0)
        cs, picks = _extract_top(cand, flat, PEER_TOPK)
        ex = [jnp.exp(c - cs[0]) for c in cs]
        den = ex[0]
        for e in ex[1:]:
            den = den + e
        for k in range(PEER_TOPK):
            row = h * PEER_TOPK + k
            pick = jnp.max(jnp.where(flat == picks[k], expert, -1.0), axis=0, keepdims=True)
            et_ref[row:row + 1, :] = pick.astype(jnp.int32)
            gt_ref[row:row + 1, :] = ex[k] / den
    idx_ref[...] = et_ref[...].T
    gate_ref[...] = gt_ref[...].T


def _peer_route(f, wq, keys1, keys2):
    t, d = f.shape
    half = PEER_D_KEY // 2
    k1 = jnp.pad(keys1, ((0, 0), (0, 0), (0, half))).astype(jnp.bfloat16)
    k2 = jnp.pad(keys2, ((0, 0), (0, 0), (half, 0))).astype(jnp.bfloat16)
    tb = ROUTE_TOKENS
    full = lambda shape: pl.BlockSpec(shape, lambda i: (0,) * len(shape))
    return pl.pallas_call(
        _route_kernel,
        grid=(t // tb,),
        in_specs=[pl.BlockSpec((tb, d), lambda i: (i, 0)), full((d, PEER_HEADS * PEER_D_KEY)),
                  full(k1.shape), full(k2.shape)],
        out_specs=[pl.BlockSpec((tb, N_SLOTS), lambda i: (i, 0)), pl.BlockSpec((tb, N_SLOTS), lambda i: (i, 0))],
        out_shape=[jax.ShapeDtypeStruct((t, N_SLOTS), jnp.int32), jax.ShapeDtypeStruct((t, N_SLOTS), jnp.float32)],
        scratch_shapes=[pltpu.VMEM((PEER_TOPK, tb), jnp.float32), pltpu.VMEM((PEER_TOPK, tb), jnp.float32),
                        pltpu.VMEM((PEER_TOPK, tb), jnp.float32), pltpu.VMEM((PEER_TOPK, tb), jnp.float32),
                        pltpu.VMEM((N_SLOTS, tb), jnp.int32), pltpu.VMEM((N_SLOTS, tb), jnp.float32)],
    )(f, wq.astype(jnp.bfloat16), k1, k2)


def _peer_u_kernel(idx_ref, x_ref, gate_ref, tab_ref, w_ref, p_ref, r_ref):
    tb = x_ref.shape[0]
    rows = N_SLOTS * SLAB

    def token(t, carry):
        x = x_ref[t]
        xlo, xhi = x[0:SLAB], x[SLAB:2 * SLAB]
        base = pl.multiple_of(t * rows, rows)
        slot_idx = idx_ref.at[t]
        for k in range(N_SLOTS):
            lo, hi = _unpack_words(tab_ref[slot_idx[k]])
            p_ref[pl.ds(base + k * SLAB, SLAB), :] = lo * xlo + hi * xhi
        return carry

    lax.fori_loop(0, tb, token, 0)

    ones = jnp.ones((8, 128), jnp.bfloat16)
    nt = (((1,), (1,)), ((), ()))
    for t in range(tb):
        part = p_ref[pl.ds(t * rows, N_SLOTS, stride=SLAB), :]
        for s in range(1, SLAB):
            part = part + p_ref[pl.ds(t * rows + s, N_SLOTS, stride=SLAB), :]
        hi = part.astype(jnp.bfloat16)
        lo = (part - hi.astype(jnp.float32)).astype(jnp.bfloat16)
        r_ref[t:t + 1, :] = (lax.dot_general(ones, hi, nt, preferred_element_type=jnp.float32)
                             + lax.dot_general(ones, lo, nt, preferred_element_type=jnp.float32))[0:1]
    r = r_ref[...]
    w_ref[...] = 0.5 * r * (1.0 + lax.erf(r * SQRT_HALF)) * gate_ref[...]


def _peer_v_kernel(idx_ref, w_ref, tab_ref, y_ref, wrep_ref):
    tb = y_ref.shape[0]
    n_acc = 4
    for t in range(tb):
        wrep_ref[t] = jnp.broadcast_to(w_ref[t:t + 1, :], (N_SLOTS, 128)).T

    def token(t, carry):
        acc_lo = [jnp.zeros((SLAB, 128), jnp.float32) for _ in range(n_acc)]
        acc_hi = [jnp.zeros((SLAB, 128), jnp.float32) for _ in range(n_acc)]
        slot_idx = idx_ref.at[t]
        for k in range(N_SLOTS):
            lo, hi = _unpack_words(tab_ref[slot_idx[k]])
            w = wrep_ref[t, pl.ds(k, 1), :]
            acc_lo[k % n_acc] = acc_lo[k % n_acc] + w * lo
            acc_hi[k % n_acc] = acc_hi[k % n_acc] + w * hi
        y_ref[t, 0:SLAB, :] = (acc_lo[0] + acc_lo[1]) + (acc_lo[2] + acc_lo[3])
        y_ref[t, SLAB:2 * SLAB, :] = (acc_hi[0] + acc_hi[1]) + (acc_hi[2] + acc_hi[3])
        return carry

    lax.fori_loop(0, tb, token, 0)


def _peer_experts(xn, idx, gate, u_tab, v_tab):
    t, d = xn.shape
    tb = PEER_TOKENS
    smem = pl.BlockSpec((tb, N_SLOTS), lambda i: (i, 0), memory_space=pltpu.SMEM)
    slots = pl.BlockSpec((tb, N_SLOTS), lambda i: (i, 0))
    resident = pl.BlockSpec(memory_space=pltpu.VMEM)
    rows3 = pl.BlockSpec((tb, 2 * SLAB, 128), lambda i: (i, 0, 0))
    params = pltpu.CompilerParams(vmem_limit_bytes=VMEM_TABLE_LIMIT)
    w = pl.pallas_call(
        _peer_u_kernel,
        grid=(t // tb,),
        in_specs=[smem, rows3, slots, resident],
        out_specs=slots,
        out_shape=jax.ShapeDtypeStruct((t, N_SLOTS), jnp.float32),
        scratch_shapes=[pltpu.VMEM((tb * N_SLOTS * SLAB, 128), jnp.float32), pltpu.VMEM((tb, N_SLOTS), jnp.float32)],
        compiler_params=params,
    )(idx, xn.reshape(t, 2 * SLAB, 128), gate, _pack_bf16_table(u_tab))
    y = pl.pallas_call(
        _peer_v_kernel,
        grid=(t // tb,),
        in_specs=[smem, slots, resident],
        out_specs=rows3,
        out_shape=jax.ShapeDtypeStruct((t, 2 * SLAB, 128), jnp.float32),
        scratch_shapes=[pltpu.VMEM((tb, N_SLOTS, 128), jnp.float32)],
        compiler_params=params,
    )(idx, w, _pack_bf16_table(v_tab))
    return y.reshape(t, d)


def peer_ffn(xn, wq, keys1, keys2, u_tab, v_tab):
    idx, gate = _peer_route(xn, wq, keys1, keys2)
    return _peer_experts(xn, idx, gate, u_tab, v_tab)


def _final_norm_kernel(x_ref, g_ref, o_ref):
    x = x_ref[...]
    y = x * lax.rsqrt(jnp.mean(x * x, axis=-1, keepdims=True) + EPS)
    o_ref[...] = y * g_ref[...]


def _final_rmsnorm(h, g):
    b, t, d = h.shape
    x = h.reshape(b * t, d)
    rows = 1024
    out = pl.pallas_call(
        _final_norm_kernel,
        grid=(b * t // rows,),
        in_specs=[pl.BlockSpec((rows, d), lambda i: (i, 0)), pl.BlockSpec((1, d), lambda i: (0, 0))],
        out_specs=pl.BlockSpec((rows, d), lambda i: (i, 0)),
        out_shape=jax.ShapeDtypeStruct((b * t, d), h.dtype),
    )(x, g.reshape(1, d))
    return out.reshape(b, t, d)


def kernel(x, c, ctx, c_ctx, ada_w, ada_b, mix_norm_g, ffn_norm_g, ev_w_in, ev_w_out,
           ret_decay_logit_f, ret_decay_logit_b, conv_w, od_w_in, od_w_out, attn_sinks,
           peer_wq, peer_keys1, peer_keys2, peer_u, peer_v, final_norm_g):
    h_lat, h_ctx = x, ctx
    silu_c = jax.nn.silu(c)
    silu_cc = jax.nn.silu(c_ctx)
    for layer in range(DEPTH):
        need_ctx = layer < DEPTH - 1
        j = layer // 2
        mod_lat = (silu_c @ ada_w[layer] + ada_b[layer])[:, None, :]
        mod_ctx = silu_cc @ ada_w[layer] + ada_b[layer]
        sh1, sc1, g1, sh2, sc2, g2 = jnp.split(mod_lat, N_ADA, axis=-1)
        csh1, csc1, cg1, csh2, csc2, cg2 = jnp.split(mod_ctx, N_ADA, axis=-1)
        a_lat = modulate(rmsnorm(h_lat, mix_norm_g[layer]), sh1, sc1)
        a_ctx = modulate(rmsnorm(h_ctx, mix_norm_g[layer]), csh1, csc1)
        if layer % 2 == 0:
            m_lat, m_ctx = even_mixer(a_lat, a_ctx, ev_w_in[j], ev_w_out[j], ret_decay_logit_f[j],
                                      ret_decay_logit_b[j], conv_w[j], need_ctx)
        else:
            m_lat, m_ctx = odd_mixer(a_lat, a_ctx, od_w_in[j], od_w_out[j], attn_sinks[j], need_ctx)
        h_lat = h_lat + g1 * m_lat
        f_lat = modulate(rmsnorm(h_lat, ffn_norm_g[layer]), sh2, sc2)
        if need_ctx:
            h_ctx = h_ctx + cg1 * m_ctx
            f_ctx = modulate(rmsnorm(h_ctx, ffn_norm_g[layer]), csh2, csc2)
            n_ctx_tok = f_ctx.shape[0] * f_ctx.shape[1]
            tokens = jnp.concatenate([f_ctx.reshape(-1, D_MODEL), f_lat.reshape(-1, D_MODEL)], axis=0)
            y = peer_ffn(tokens, peer_wq[layer], peer_keys1[layer], peer_keys2[layer], peer_u[layer], peer_v[layer])
            h_ctx = h_ctx + cg2 * y[:n_ctx_tok].reshape(h_ctx.shape)
            y_lat = y[n_ctx_tok:].reshape(h_lat.shape)
        else:
            y_lat = peer_ffn(f_lat.reshape(-1, D_MODEL), peer_wq[layer], peer_keys1[layer], peer_keys2[layer],
                             peer_u[layer], peer_v[layer]).reshape(h_lat.shape)
        h_lat = h_lat + g2 * y_lat
    return _final_rmsnorm(h_lat, final_norm_g)
```

```python
import functools

import numpy as np
import jax
import jax.numpy as jnp
from jax import lax
from jax.experimental import pallas as pl
from jax.experimental.pallas import tpu as pltpu

D_MODEL = 1024
DEPTH = 2
GRID_W = 64
EPS = 1e-6
ROPE_BASE = 10000.0
NEG_INF = -1e30
N_ADA = 6
RET_HEADS = 4
RET_V_DIM = D_MODEL // (2 * RET_HEADS)
RET_QK_DIM = RET_V_DIM // 2
RET_CHUNK = 128
CONV_CH = D_MODEL // 2
CONV_K = 3
RET_Q_W = RET_HEADS * RET_QK_DIM
RET_V_W = RET_HEADS * RET_V_DIM
EV_COLS = 2 * RET_Q_W + 2 * RET_V_W + 3 * CONV_CH
ATT_HEADS = 16
ATT_HEAD_DIM = D_MODEL // ATT_HEADS
ATT_KV_HEADS = 4
ATT_GROUP = ATT_HEADS // ATT_KV_HEADS
WINDOW = 128
ATT_BLOCK = 128
ATT_Q_W = ATT_HEADS * ATT_HEAD_DIM
ATT_KV_W = ATT_KV_HEADS * ATT_HEAD_DIM
PEER_HEADS = 8
PEER_N_KEYS = 128
PEER_D_KEY = 128
PEER_TOPK = 16
ROUTE_TOKENS = 128
PEER_TOKENS = 32
N_SLOTS = PEER_HEADS * PEER_TOPK
HALF_D = D_MODEL // 2
SLAB = HALF_D // 128
BF16_HI_MASK = -65536
VMEM_TABLE_LIMIT = 48 * 1024 * 1024
VMEM_MIXER_LIMIT = 40 * 1024 * 1024
ID_BIG = 1e9
SQRT_HALF = 0.7071067811865476
ROW_TILE = 256
ADA_ROWS = 40

NT = (((1,), (1,)), ((), ()))
TN = (((0,), (0,)), ((), ()))
BF = jnp.bfloat16
F32 = jnp.float32


def _full(shape):
    return pl.BlockSpec(shape, lambda *_: (0,) * len(shape))


def _mm(a, b):
    return jnp.dot(a.astype(BF), b.astype(BF), preferred_element_type=F32)


def _rope_tables(n_tok, head_dim, n_heads):
    n_rows = n_tok // GRID_W
    rows = jnp.broadcast_to(jnp.arange(n_rows, dtype=F32)[:, None], (n_rows, GRID_W)).reshape(-1)
    cols = jnp.broadcast_to(jnp.arange(GRID_W, dtype=F32)[None, :], (n_rows, GRID_W)).reshape(-1)
    n_freq = head_dim // 4
    inv_freq = ROPE_BASE ** (-jnp.arange(n_freq, dtype=F32) / n_freq)
    ar, ac = rows[:, None] * inv_freq, cols[:, None] * inv_freq
    cos = jnp.concatenate([jnp.cos(ar), jnp.cos(ar), jnp.cos(ac), jnp.cos(ac)], axis=-1)
    sin = jnp.concatenate([-jnp.sin(ar), jnp.sin(ar), -jnp.sin(ac), jnp.sin(ac)], axis=-1)
    return jnp.tile(cos, (1, n_heads)), jnp.tile(sin, (1, n_heads))


def _swap_columns(n_heads, head_dim):
    q = head_dim // 4
    base = np.concatenate([np.arange(q, 2 * q), np.arange(0, q), np.arange(3 * q, 4 * q), np.arange(2 * q, 3 * q)])
    return (np.arange(n_heads)[:, None] * head_dim + base[None, :]).reshape(-1)


def _ada_kernel(c_ref, w_ref, b_ref, o_ref):
    c = c_ref[...]
    s = c / (1.0 + jnp.exp(-c))
    o_ref[...] = _mm(s, w_ref[...]) + b_ref[...]


def _ada_modulation(c_rows, w, b):
    d, f = w.shape
    tn = f // 4
    return pl.pallas_call(
        _ada_kernel,
        grid=(f // tn,),
        in_specs=[_full((ADA_ROWS, d)), pl.BlockSpec((d, tn), lambda j: (0, j)), pl.BlockSpec((1, tn), lambda j: (0, j))],
        out_specs=pl.BlockSpec((ADA_ROWS, tn), lambda j: (0, j)),
        out_shape=jax.ShapeDtypeStruct((ADA_ROWS, f), F32),
    )(c_rows, w.astype(BF), b.reshape(1, f))


def _norm_mod(x, gain, shift, scale):
    n = x * lax.rsqrt(jnp.mean(x * x, axis=-1, keepdims=True) + EPS) * gain
    return n * (1.0 + scale) + shift


def _proj_kernel(*refs, has_add, emit_h):
    refs = list(refs)
    h_ref = refs.pop(0)
    x = h_ref[...]
    if has_add:
        y_ref, gt_ref = refs.pop(0), refs.pop(0)
        x = x + gt_ref[0] * y_ref[...]
    g_ref, sh_ref, sc_ref, w_ref = refs[:4]
    outs = refs[4:]
    if emit_h:
        outs.pop(0)[...] = x
    a = _norm_mod(x, g_ref[...], sh_ref[0], sc_ref[0])
    outs[0][...] = _mm(a, w_ref[...])


def _in_projection(h, gain, shift, scale, w, rows_per_mod, add=None, emit_h=False):
    n, d = h.shape
    f = w.shape[1]
    tm = ROW_TILE
    rows = pl.BlockSpec((tm, d), lambda i: (i, 0))
    mod = pl.BlockSpec((1, 1, d), lambda i: ((i * tm) // rows_per_mod, 0, 0))
    args, specs = [h], [rows]
    if add is not None:
        args += [add[0], add[1]]
        specs += [rows, mod]
    args += [gain.reshape(1, d), shift, scale, w.astype(BF)]
    specs += [_full((1, d)), mod, mod, _full((d, f))]
    out_shape = [jax.ShapeDtypeStruct((n, f), F32)]
    out_specs = [pl.BlockSpec((tm, f), lambda i: (i, 0))]
    if emit_h:
        out_shape.insert(0, jax.ShapeDtypeStruct((n, d), F32))
        out_specs.insert(0, rows)
    res = pl.pallas_call(
        functools.partial(_proj_kernel, has_add=add is not None, emit_h=emit_h),
        grid=(n // tm,), in_specs=specs, out_specs=out_specs, out_shape=out_shape,
        compiler_params=pltpu.CompilerParams(vmem_limit_bytes=VMEM_MIXER_LIMIT),
    )(*args)
    return res if emit_h else res[0]


def _retention_kernel(logf_ref, logb_ref, ql_ref, kl_ref, vl_ref, qs_ref, ks_ref, qc_ref, kc_ref, vc_ref,
                      cos_ref, sin_ref, ol_ref, oc_ref):
    c = RET_CHUNK
    n_lat = ql_ref.shape[0] // c
    n_ctx = qc_ref.shape[0] // c
    k_scale = RET_QK_DIM ** -0.5
    row = lax.broadcasted_iota(jnp.int32, (c, c), 0).astype(F32)
    col = lax.broadcasted_iota(jnp.int32, (c, c), 1).astype(F32)
    rowk = lax.broadcasted_iota(jnp.int32, (c, RET_QK_DIM), 0).astype(F32)

    def step(qh, kh, vh, state, dmat, xi, zeta, cdm):
        qb, vb = qh.astype(BF), vh.astype(BF)
        scores = lax.dot_general(qb, kh.astype(BF), NT, preferred_element_type=F32) * dmat
        out = _mm(scores, vb) + _mm(qb, state) * xi
        kv = lax.dot_general((kh * zeta).astype(BF), vb, TN, preferred_element_type=F32)
        return out, cdm * state + kv

    for backward in (False, True):
        for h in range(RET_HEADS):
            lg = (logb_ref if backward else logf_ref)[h]
            if backward:
                dmat = jnp.where(col >= row, jnp.exp(lg * (col - row)), 0.0)
                xi = jnp.exp(lg * (c - row))
                zeta = jnp.exp(lg * rowk)
            else:
                dmat = jnp.where(row >= col, jnp.exp(lg * (row - col)), 0.0)
                xi = jnp.exp(lg * (row + 1.0))
                zeta = jnp.exp(lg * (c - 1.0 - rowk))
            cdm = jnp.exp(jnp.full((RET_QK_DIM, RET_V_DIM), lg * c, F32))
            qk = slice(h * RET_QK_DIM, (h + 1) * RET_QK_DIM)
            vv = slice(h * RET_V_DIM, (h + 1) * RET_V_DIM)
            state = jnp.zeros((RET_QK_DIM, RET_V_DIM), F32)
            for n in (range(n_ctx - 1, -1, -1) if backward else range(n_ctx)):
                r = slice(n * c, (n + 1) * c)
                out, state = step(qc_ref[r, qk], kc_ref[r, qk] * k_scale, vc_ref[r, vv], state, dmat, xi, zeta, cdm)
                oc_ref[r, vv] = oc_ref[r, vv] + out if backward else out

            def lat_chunk(i, state):
                n = (n_lat - 1 - i) if backward else i
                r = pl.ds(pl.multiple_of(n * c, c), c)
                cs, sn = cos_ref[r, qk], sin_ref[r, qk]
                qh = ql_ref[r, qk] * cs + qs_ref[r, qk] * sn
                kh = (kl_ref[r, qk] * cs + ks_ref[r, qk] * sn) * k_scale
                out, state = step(qh, kh, vl_ref[r, vv], state, dmat, xi, zeta, cdm)
                ol_ref[r, vv] = ol_ref[r, vv] + out if backward else out
                return state

            lax.fori_loop(0, n_lat, lat_chunk, state)


def _retention(p_lat, p_ctx, log_f, log_b, cos, sin, batch):
    s = p_lat.shape[0] // batch
    c = p_ctx.shape[0] // batch
    qw = RET_Q_W
    swap0 = EV_COLS // qw
    smem = pl.BlockSpec(memory_space=pltpu.SMEM)
    lat = lambda width, j: pl.BlockSpec((s, width), lambda b: (b, j))
    ctx = lambda width, j: pl.BlockSpec((c, width), lambda b: (b, j))
    return pl.pallas_call(
        _retention_kernel,
        grid=(batch,),
        in_specs=[smem, smem, lat(qw, 0), lat(qw, 1), lat(RET_V_W, 1), lat(qw, swap0), lat(qw, swap0 + 1),
                  ctx(qw, 0), ctx(qw, 1), ctx(RET_V_W, 1), _full((s, qw)), _full((s, qw))],
        out_specs=[lat(RET_V_W, 0), ctx(RET_V_W, 0)],
        out_shape=[jax.ShapeDtypeStruct((batch * s, RET_V_W), F32), jax.ShapeDtypeStruct((batch * c, RET_V_W), F32)],
        compiler_params=pltpu.CompilerParams(vmem_limit_bytes=VMEM_MIXER_LIMIT),
    )(log_f, log_b, p_lat, p_lat, p_lat, p_lat, p_lat, p_ctx, p_ctx, p_ctx, cos, sin)


def _even_out_kernel(h_ref, gate_ref, ret_ref, g_ref, gb_ref, gc_ref, x_ref, gcp_ref, xp_ref, gcn_ref, xn_ref,
                     cw_ref, w_ref, o_ref, *, seq_blocks):
    i = pl.program_id(0)
    tm = h_ref.shape[0]
    u = gc_ref[...] * x_ref[...]
    seq_pos = i % seq_blocks
    keep_prev = jnp.where(seq_pos == 0, 0.0, 1.0)
    keep_next = jnp.where(seq_pos == seq_blocks - 1, 0.0, 1.0)
    u_prev = gcp_ref[7:8, :] * xp_ref[7:8, :] * keep_prev
    u_next = gcn_ref[0:1, :] * xn_ref[0:1, :] * keep_next
    rid = lax.broadcasted_iota(jnp.int32, u.shape, 0)
    up = jnp.where(rid == 0, u_prev, pltpu.roll(u, 1, 0))
    un = jnp.where(rid == tm - 1, u_next, pltpu.roll(u, tm - 1, 0))
    conv = cw_ref[0:1, :] * up + cw_ref[1:2, :] * u + cw_ref[2:3, :] * un
    parts = []
    for hh in range(RET_HEADS):
        lanes = slice(hh * RET_V_DIM, (hh + 1) * RET_V_DIM)
        r = ret_ref[:, lanes]
        g = g_ref[:, lanes]
        parts.append(r * lax.rsqrt(jnp.mean(r * r, axis=-1, keepdims=True) + EPS) * (g / (1.0 + jnp.exp(-g))))
    y = jnp.concatenate(parts + [gb_ref[...] * conv], axis=1)
    o_ref[...] = h_ref[...] + gate_ref[0] * _mm(y, w_ref[...])


def _even_output(h, gate, ret, p, conv_w, w_out, seq_len, rows_per_mod):
    n, d = h.shape
    tm = ROW_TILE
    cw = CONV_CH
    tiles = tm // 8
    last_tile = n // 8 - 1
    rows = lambda width, j: pl.BlockSpec((tm, width), lambda i: (i, j))
    prev = lambda j: pl.BlockSpec((8, cw), lambda i: (jnp.maximum(i * tiles - 1, 0), j))
    nxt = lambda j: pl.BlockSpec((8, cw), lambda i: (jnp.minimum((i + 1) * tiles, last_tile), j))
    mod = pl.BlockSpec((1, 1, d), lambda i: ((i * tm) // rows_per_mod, 0, 0))
    return pl.pallas_call(
        functools.partial(_even_out_kernel, seq_blocks=seq_len // tm),
        grid=(n // tm,),
        in_specs=[rows(d, 0), mod, rows(RET_V_W, 0), rows(cw, 2), rows(cw, 3), rows(cw, 4), rows(cw, 5),
                  prev(4), prev(5), nxt(4), nxt(5), _full((CONV_K, cw)), _full((d, d))],
        out_specs=rows(d, 0),
        out_shape=jax.ShapeDtypeStruct((n, d), F32),
    )(h, gate, ret, p, p, p, p, p, p, p, p, conv_w, w_out.astype(BF))


def _attn_kernel(sink_ref, h_ref, gate_ref, q_ref, qs_ref, kp_ref, kc_ref, kn_ref, ksp_ref, ksc_ref, ksn_ref,
                 vp_ref, vc_ref, vn_ref, kx_ref, vx_ref, cq_ref, sq_ref, ckp_ref, ckc_ref, ckn_ref,
                 skp_ref, skc_ref, skn_ref, w_ref, o_ref, *, n_lat):
    j = pl.program_id(1)
    blk = ATT_BLOCK
    span = blk + 2 * WINDOW
    n_keys = span + kx_ref.shape[0]
    scale = ATT_HEAD_DIM ** -0.5
    q = (q_ref[...] * cq_ref[...] + qs_ref[...] * sq_ref[...]) * scale
    keys = jnp.concatenate([kp_ref[...] * ckp_ref[...] + ksp_ref[...] * skp_ref[...],
                            kc_ref[...] * ckc_ref[...] + ksc_ref[...] * skc_ref[...],
                            kn_ref[...] * ckn_ref[...] + ksn_ref[...] * skn_ref[...],
                            kx_ref[...]], axis=0).astype(BF)
    vals = jnp.concatenate([vp_ref[...], vc_ref[...], vn_ref[...], vx_ref[...]], axis=0).astype(BF)
    qpos = lax.broadcasted_iota(jnp.int32, (blk, n_keys), 0)
    r = lax.broadcasted_iota(jnp.int32, (blk, n_keys), 1)
    key_pos = (j - 1) * blk + r
    in_band = (r >= qpos) & (r <= qpos + 2 * WINDOW) & (key_pos >= 0) & (key_pos < n_lat)
    valid = jnp.concatenate([in_band | (r >= span)] * ATT_GROUP, axis=0)
    heads = [None] * ATT_HEADS
    for kh in range(ATT_KV_HEADS):
        kv = slice(kh * ATT_HEAD_DIM, (kh + 1) * ATT_HEAD_DIM)
        qg = jnp.concatenate([q[:, (kh * ATT_GROUP + g) * ATT_HEAD_DIM:(kh * ATT_GROUP + g + 1) * ATT_HEAD_DIM]
                              for g in range(ATT_GROUP)], axis=0)
        s = lax.dot_general(qg.astype(BF), keys[:, kv], NT, preferred_element_type=F32)
        s = jnp.where(valid, s, NEG_INF)
        sink = jnp.concatenate([jnp.full((blk, 1), sink_ref[kh * ATT_GROUP + g], F32) for g in range(ATT_GROUP)], axis=0)
        m = jnp.maximum(jnp.max(s, axis=-1, keepdims=True), sink)
        p = jnp.exp(s - m)
        den = jnp.sum(p, axis=-1, keepdims=True) + jnp.exp(sink - m)
        o = _mm(p, vals[:, kv]) / den
        for g in range(ATT_GROUP):
            heads[kh * ATT_GROUP + g] = o[g * blk:(g + 1) * blk]
    o_ref[...] = h_ref[...] + gate_ref[0] * _mm(jnp.concatenate(heads, axis=1), w_ref[...])


def _attention(h, gate, p_lat, p_ctx, sinks, cos, sin, w_out, batch):
    n, d = h.shape
    s = n // batch
    c = p_ctx.shape[0] // batch
    blk = ATT_BLOCK
    nb = s // blk
    kw = ATT_KV_W
    k0 = 2 * ATT_Q_W // kw
    row = lambda width, col: pl.BlockSpec((blk, width), lambda b, j: (b * nb + j, col))
    prv = lambda width, col: pl.BlockSpec((blk, width), lambda b, j: (b * nb + jnp.maximum(j - 1, 0), col))
    nxt = lambda width, col: pl.BlockSpec((blk, width), lambda b, j: (b * nb + jnp.minimum(j + 1, nb - 1), col))
    tab = lambda width: pl.BlockSpec((blk, width), lambda b, j: (j, 0))
    tab_p = lambda width: pl.BlockSpec((blk, width), lambda b, j: (jnp.maximum(j - 1, 0), 0))
    tab_n = lambda width: pl.BlockSpec((blk, width), lambda b, j: (jnp.minimum(j + 1, nb - 1), 0))
    ctx = lambda col: pl.BlockSpec((c, kw), lambda b, j: (b, col))
    mod = pl.BlockSpec((1, 1, d), lambda b, j: (b, 0, 0))
    return pl.pallas_call(
        functools.partial(_attn_kernel, n_lat=s),
        grid=(batch, nb),
        in_specs=[pl.BlockSpec(memory_space=pltpu.SMEM), row(d, 0), mod, row(ATT_Q_W, 0), row(ATT_Q_W, 1),
                  prv(kw, k0), row(kw, k0), nxt(kw, k0), prv(kw, k0 + 1), row(kw, k0 + 1), nxt(kw, k0 + 1),
                  prv(kw, k0 + 2), row(kw, k0 + 2), nxt(kw, k0 + 2), ctx(0), ctx(1),
                  tab(ATT_Q_W), tab(ATT_Q_W), tab_p(kw), tab(kw), tab_n(kw), tab_p(kw), tab(kw), tab_n(kw),
                  pl.BlockSpec((d, d), lambda b, j: (0, 0))],
        out_specs=row(d, 0),
        out_shape=jax.ShapeDtypeStruct((n, d), F32),
    )(sinks, h, gate, p_lat, p_lat, p_lat, p_lat, p_lat, p_lat, p_lat, p_lat, p_lat, p_lat, p_lat, p_ctx, p_ctx,
      cos, sin, cos, cos, cos, sin, sin, sin, w_out.astype(BF))


def _pack_bf16_table(tab):
    e = tab.shape[0]
    bits = lax.bitcast_convert_type(tab.astype(BF), jnp.uint16).astype(jnp.uint32)
    word = (bits[:, HALF_D:] << 16) | bits[:, :HALF_D]
    return lax.bitcast_convert_type(word, jnp.int32).reshape(e, SLAB, 128)


def _unpack_words(words):
    lo = lax.bitcast_convert_type(words << 16, F32)
    hi = lax.bitcast_convert_type(words & BF16_HI_MASK, F32)
    return lo, hi


def _extract_top(s, order, n):
    vals, ids = [], []
    for _ in range(n):
        m = jnp.max(s, axis=0, keepdims=True)
        first = jnp.min(jnp.where(s == m, order, ID_BIG), axis=0, keepdims=True)
        vals.append(m)
        ids.append(first)
        s = jnp.where(order == first, -jnp.inf, s)
    return vals, ids


def _route_kernel(h_ref, g_ref, sh_ref, sc_ref, wq_ref, k1_ref, k2_ref, f_ref, idx_ref, gate_ref,
                  v1_ref, v2_ref, i1_ref, i2_ref, et_ref, gt_ref):
    tb = h_ref.shape[0]
    f = _norm_mod(h_ref[...], g_ref[...], sh_ref[0], sc_ref[0])
    f_ref[...] = f
    qb = _mm(f, wq_ref[...]).astype(BF)
    key_id = lax.broadcasted_iota(jnp.int32, (PEER_N_KEYS, tb), 0).astype(F32)
    r8 = lax.broadcasted_iota(jnp.int32, (8, tb), 0).astype(F32)
    flat = jnp.concatenate([r8 * 16, (r8 + 8) * 16, r8 * 16 + 1, r8 + 8, r8, r8 + 16, r8 + 32, r8 + 48, r8 + 64], axis=0)
    for h in range(PEER_HEADS):
        qh = qb[:, h * PEER_D_KEY:(h + 1) * PEER_D_KEY]
        s1 = lax.dot_general(k1_ref[h], qh, NT, preferred_element_type=F32)
        s2 = lax.dot_general(k2_ref[h], qh, NT, preferred_element_type=F32)
        for s, v_ref, i_ref in ((s1, v1_ref, i1_ref), (s2, v2_ref, i2_ref)):
            vals, ids = _extract_top(s, key_id, PEER_TOPK)
            for k in range(PEER_TOPK):
                v_ref[k:k + 1, :] = vals[k]
                i_ref[k:k + 1, :] = ids[k]
        v1a, v1b, v2a, v2b = v1_ref[0:8, :], v1_ref[8:16, :], v2_ref[0:8, :], v2_ref[8:16, :]
        i1a, i1b, i2a, i2b = i1_ref[0:8, :], i1_ref[8:16, :], i2_ref[0:8, :], i2_ref[8:16, :]
        ninf = -jnp.inf
        cand = jnp.concatenate([
            v1a + v2a[0:1], v1b + v2a[0:1], v1a + v2a[1:2], v2b + v1a[0:1],
            jnp.where(r8 >= 2, v2a + v1a[0:1], ninf),
            jnp.where(r8 >= 2, v2a + v1a[1:2], ninf),
            jnp.where((r8 >= 2) & (r8 <= 4), v2a + v1a[2:3], ninf),
            jnp.where((r8 >= 2) & (r8 <= 3), v2a + v1a[3:4], ninf),
            jnp.where(r8 == 2, v2a + v1a[4:5], ninf)], axis=0)
        expert = jnp.concatenate([
            i1a * PEER_N_KEYS + i2a[0:1], i1b * PEER_N_KEYS + i2a[0:1], i1a * PEER_N_KEYS + i2a[1:2],
            i1a[0:1] * PEER_N_KEYS + i2b,
            i1a[0:1] * PEER_N_KEYS + i2a, i1a[1:2] * PEER_N_KEYS + i2a, i1a[2:3] * PEER_N_KEYS + i2a,
            i1a[3:4] * PEER_N_KEYS + i2a, i1a[4:5] * PEER_N_KEYS + i2a], axis=0)
        cs, picks = _extract_top(cand, flat, PEER_TOPK)
        ex = [jnp.exp(c - cs[0]) for c in cs]
        den = ex[0]
        for e in ex[1:]:
            den = den + e
        for k in range(PEER_TOPK):
            row = h * PEER_TOPK + k
            pick = jnp.max(jnp.where(flat == picks[k], expert, -1.0), axis=0, keepdims=True)
            et_ref[row:row + 1, :] = pick.astype(jnp.int32)
            gt_ref[row:row + 1, :] = ex[k] / den
    idx_ref[...] = et_ref[...].T
    gate_ref[...] = gt_ref[...].T


def _peer_route(h, gain, shift, scale, rows_per_mod, wq, k1, k2):
    t, d = h.shape
    tb = ROUTE_TOKENS
    mod = pl.BlockSpec((1, 1, d), lambda i: ((i * tb) // rows_per_mod, 0, 0))
    slots = pl.BlockSpec((tb, N_SLOTS), lambda i: (i, 0))
    return pl.pallas_call(
        _route_kernel,
        grid=(t // tb,),
        in_specs=[pl.BlockSpec((tb, d), lambda i: (i, 0)), _full((1, d)), mod, mod,
                  _full((d, PEER_HEADS * PEER_D_KEY)), _full(k1.shape), _full(k2.shape)],
        out_specs=[pl.BlockSpec((tb, d), lambda i: (i, 0)), slots, slots],
        out_shape=[jax.ShapeDtypeStruct((t, d), F32), jax.ShapeDtypeStruct((t, N_SLOTS), jnp.int32),
                   jax.ShapeDtypeStruct((t, N_SLOTS), F32)],
        scratch_shapes=[pltpu.VMEM((PEER_TOPK, tb), F32), pltpu.VMEM((PEER_TOPK, tb), F32),
                        pltpu.VMEM((PEER_TOPK, tb), F32), pltpu.VMEM((PEER_TOPK, tb), F32),
                        pltpu.VMEM((N_SLOTS, tb), jnp.int32), pltpu.VMEM((N_SLOTS, tb), F32)],
    )(h, gain.reshape(1, d), shift, scale, wq, k1, k2)


def _peer_u_kernel(idx_ref, x_ref, gate_ref, tab_ref, w_ref, p_ref, r_ref):
    tb = x_ref.shape[0]
    rows = N_SLOTS * SLAB

    def token(t, carry):
        x = x_ref[t]
        xlo, xhi = x[0:SLAB], x[SLAB:2 * SLAB]
        base = pl.multiple_of(t * rows, rows)
        slot_idx = idx_ref.at[t]
        for k in range(N_SLOTS):
            lo, hi = _unpack_words(tab_ref[slot_idx[k]])
            p_ref[pl.ds(base + k * SLAB, SLAB), :] = lo * xlo + hi * xhi
        return carry

    lax.fori_loop(0, tb, token, 0)

    ones = jnp.ones((8, 128), BF)
    for t in range(tb):
        part = p_ref[pl.ds(t * rows, N_SLOTS, stride=SLAB), :]
        for s in range(1, SLAB):
            part = part + p_ref[pl.ds(t * rows + s, N_SLOTS, stride=SLAB), :]
        hi = part.astype(BF)
        lo = (part - hi.astype(F32)).astype(BF)
        r_ref[t:t + 1, :] = (lax.dot_general(ones, hi, NT, preferred_element_type=F32)
                             + lax.dot_general(ones, lo, NT, preferred_element_type=F32))[0:1]
    r = r_ref[...]
    w_ref[...] = 0.5 * r * (1.0 + lax.erf(r * SQRT_HALF)) * gate_ref[...]


def _peer_v_kernel(idx_ref, w_ref, tab_ref, y_ref, wrep_ref):
    tb = y_ref.shape[0]
    n_acc = 4
    for t in range(tb):
        wrep_ref[t] = jnp.broadcast_to(w_ref[t:t + 1, :], (N_SLOTS, 128)).T

    def token(t, carry):
        acc_lo = [jnp.zeros((SLAB, 128), F32) for _ in range(n_acc)]
        acc_hi = [jnp.zeros((SLAB, 128), F32) for _ in range(n_acc)]
        slot_idx = idx_ref.at[t]
        for k in range(N_SLOTS):
            lo, hi = _unpack_words(tab_ref[slot_idx[k]])
            w = wrep_ref[t, pl.ds(k, 1), :]
            acc_lo[k % n_acc] = acc_lo[k % n_acc] + w * lo
            acc_hi[k % n_acc] = acc_hi[k % n_acc] + w * hi
        y_ref[t, 0:SLAB, :] = (acc_lo[0] + acc_lo[1]) + (acc_lo[2] + acc_lo[3])
        y_ref[t, SLAB:2 * SLAB, :] = (acc_hi[0] + acc_hi[1]) + (acc_hi[2] + acc_hi[3])
        return carry

    lax.fori_loop(0, tb, token, 0)


def _peer_experts(f, idx, gate, u_words, v_words):
    t, d = f.shape
    tb = PEER_TOKENS
    smem = pl.BlockSpec((tb, N_SLOTS), lambda i: (i, 0), memory_space=pltpu.SMEM)
    slots = pl.BlockSpec((tb, N_SLOTS), lambda i: (i, 0))
    resident = pl.BlockSpec(memory_space=pltpu.VMEM)
    rows3 = pl.BlockSpec((tb, 2 * SLAB, 128), lambda i: (i, 0, 0))
    params = pltpu.CompilerParams(vmem_limit_bytes=VMEM_TABLE_LIMIT)
    w = pl.pallas_call(
        _peer_u_kernel,
        grid=(t // tb,),
        in_specs=[smem, rows3, slots, resident],
        out_specs=slots,
        out_shape=jax.ShapeDtypeStruct((t, N_SLOTS), F32),
        scratch_shapes=[pltpu.VMEM((tb * N_SLOTS * SLAB, 128), F32), pltpu.VMEM((tb, N_SLOTS), F32)],
        compiler_params=params,
    )(idx, f.reshape(t, 2 * SLAB, 128), gate, u_words)
    y = pl.pallas_call(
        _peer_v_kernel,
        grid=(t // tb,),
        in_specs=[smem, slots, resident],
        out_specs=rows3,
        out_shape=jax.ShapeDtypeStruct((t, 2 * SLAB, 128), F32),
        scratch_shapes=[pltpu.VMEM((tb, N_SLOTS, 128), F32)],
        compiler_params=params,
    )(idx, w, v_words)
    return y.reshape(t, d)


class _PeerWeights:
    def __init__(self, wq, keys1, keys2, u_tab, v_tab):
        half = PEER_D_KEY // 2
        self.wq = wq.astype(BF)
        self.k1 = jnp.pad(keys1, ((0, 0), (0, 0), (0, half))).astype(BF)
        self.k2 = jnp.pad(keys2, ((0, 0), (0, 0), (half, 0))).astype(BF)
        self.u = _pack_bf16_table(u_tab)
        self.v = _pack_bf16_table(v_tab)


def _peer_ffn(h, gain, shift, scale, rows_per_mod, pw):
    f, idx, gate = _peer_route(h, gain, shift, scale, rows_per_mod, pw.wq, pw.k1, pw.k2)
    return _peer_experts(f, idx, gate, pw.u, pw.v)


def _final_kernel(h_ref, y_ref, gate_ref, g_ref, o_ref):
    x = h_ref[...] + gate_ref[0] * y_ref[...]
    o_ref[...] = x * lax.rsqrt(jnp.mean(x * x, axis=-1, keepdims=True) + EPS) * g_ref[...]


def _final_norm(h, y, gate, gain, rows_per_mod):
    n, d = h.shape
    tm = ROW_TILE
    rows = pl.BlockSpec((tm, d), lambda i: (i, 0))
    mod = pl.BlockSpec((1, 1, d), lambda i: ((i * tm) // rows_per_mod, 0, 0))
    return pl.pallas_call(
        _final_kernel, grid=(n // tm,), in_specs=[rows, rows, mod, _full((1, d))], out_specs=rows,
        out_shape=jax.ShapeDtypeStruct((n, d), F32),
    )(h, y, gate, gain.reshape(1, d))


def kernel(x, c, ctx, c_ctx, ada_w, ada_b, mix_norm_g, ffn_norm_g, ev_w_in, ev_w_out,
           ret_decay_logit_f, ret_decay_logit_b, conv_w, od_w_in, od_w_out, attn_sinks,
           peer_wq, peer_keys1, peer_keys2, peer_u, peer_v, final_norm_g):
    batch, s, d = x.shape
    n_ctx = ctx.shape[1]
    assert DEPTH == 2 and s % ROW_TILE == 0 and n_ctx % ROW_TILE == 0 and batch + 1 <= ADA_ROWS
    h_lat = x.reshape(batch * s, d)
    h_ctx = ctx.reshape(batch * n_ctx, d)
    c_rows = jnp.zeros((ADA_ROWS, d), F32).at[:batch].set(c).at[batch].set(c_ctx)

    def modulation(layer):
        mod = _ada_modulation(c_rows, ada_w[layer], ada_b[layer])
        lat = [m.reshape(batch, 1, d) for m in jnp.split(mod[:batch], N_ADA, axis=-1)]
        cx = [m.reshape(1, 1, d) for m in jnp.split(mod[batch:batch + 1], N_ADA, axis=-1)]
        return lat, cx

    (sh1, sc1, g1, sh2, sc2, g2), (csh1, csc1, cg1, csh2, csc2, cg2) = modulation(0)
    w_in = ev_w_in[0]
    swap = _swap_columns(RET_HEADS, RET_QK_DIM)
    w0 = jnp.concatenate([w_in, w_in[:, :RET_Q_W][:, swap], w_in[:, RET_Q_W:2 * RET_Q_W][:, swap]], axis=1)
    p_lat = _in_projection(h_lat, mix_norm_g[0], sh1, sc1, w0, s)
    p_ctx = _in_projection(h_ctx, mix_norm_g[0], csh1, csc1, w0, batch * n_ctx)
    log_f = jax.nn.log_sigmoid(ret_decay_logit_f[0].astype(F32))
    log_b = jax.nn.log_sigmoid(ret_decay_logit_b[0].astype(F32))
    cos, sin = _rope_tables(s, RET_QK_DIM, RET_HEADS)
    ret_lat, ret_ctx = _retention(p_lat, p_ctx, log_f, log_b, cos, sin, batch)
    h_lat = _even_output(h_lat, g1, ret_lat, p_lat, conv_w[0], ev_w_out[0], s, s)
    h_ctx = _even_output(h_ctx, cg1, ret_ctx, p_ctx, conv_w[0], ev_w_out[0], n_ctx, batch * n_ctx)
    pw = _PeerWeights(peer_wq[0], peer_keys1[0], peer_keys2[0], peer_u[0], peer_v[0])
    y_lat = _peer_ffn(h_lat, ffn_norm_g[0], sh2, sc2, s, pw)
    y_ctx = _peer_ffn(h_ctx, ffn_norm_g[0], csh2, csc2, batch * n_ctx, pw)

    (sh1, sc1, g1, sh2, sc2, g2b), (csh1, csc1, _, _, _, _) = modulation(1)
    w_in = od_w_in[0]
    wq_cols, wk_cols, wv_cols = w_in[:, :ATT_Q_W], w_in[:, ATT_Q_W:ATT_Q_W + ATT_KV_W], w_in[:, ATT_Q_W + ATT_KV_W:]
    w1 = jnp.concatenate([wq_cols, wq_cols[:, _swap_columns(ATT_HEADS, ATT_HEAD_DIM)], wk_cols,
                          wk_cols[:, _swap_columns(ATT_KV_HEADS, ATT_HEAD_DIM)], wv_cols], axis=1)
    h_lat, p_lat = _in_projection(h_lat, mix_norm_g[1], sh1, sc1, w1, s, add=(y_lat, g2), emit_h=True)
    p_ctx = _in_projection(h_ctx, mix_norm_g[1], csh1, csc1, w_in[:, ATT_Q_W:], batch * n_ctx, add=(y_ctx, cg2))
    cos, sin = _rope_tables(s, ATT_HEAD_DIM, ATT_HEADS)
    h_lat = _attention(h_lat, g1, p_lat, p_ctx, attn_sinks[0].astype(F32), cos, sin, od_w_out[0], batch)
    pw = _PeerWeights(peer_wq[1], peer_keys1[1], peer_keys2[1], peer_u[1], peer_v[1])
    y_lat = _peer_ffn(h_lat, ffn_norm_g[1], sh2, sc2, s, pw)
    out = _final_norm(h_lat, y_lat, g2b, final_norm_g, s)
    return out.reshape(batch, s, d)
```

```python
import functools

import numpy as np
import jax
import jax.numpy as jnp
from jax import lax
from jax.experimental import pallas as pl
from jax.experimental.pallas import tpu as pltpu

D_MODEL = 1024
DEPTH = 2
GRID_W = 64
EPS = 1e-6
ROPE_BASE = 10000.0
NEG_INF = -1e30
N_ADA = 6
RET_HEADS = 4
RET_V_DIM = D_MODEL // (2 * RET_HEADS)
RET_QK_DIM = RET_V_DIM // 2
RET_CHUNK = 128
CONV_CH = D_MODEL // 2
CONV_K = 3
RET_Q_W = RET_HEADS * RET_QK_DIM
RET_V_W = RET_HEADS * RET_V_DIM
EV_COLS = 2 * RET_Q_W + 2 * RET_V_W + 3 * CONV_CH
ATT_HEADS = 16
ATT_HEAD_DIM = D_MODEL // ATT_HEADS
ATT_KV_HEADS = 4
ATT_GROUP = ATT_HEADS // ATT_KV_HEADS
WINDOW = 128
ATT_BLOCK = 128
ATT_Q_W = ATT_HEADS * ATT_HEAD_DIM
ATT_KV_W = ATT_KV_HEADS * ATT_HEAD_DIM
PEER_HEADS = 8
PEER_N_KEYS = 128
PEER_D_KEY = 128
PEER_TOPK = 16
ROUTE_TOKENS = 128
PEER_TOKENS = 32
N_SLOTS = PEER_HEADS * PEER_TOPK
HALF_D = D_MODEL // 2
SLAB = HALF_D // 128
BF16_HI_MASK = -65536
VMEM_TABLE_LIMIT = 48 * 1024 * 1024
VMEM_MIXER_LIMIT = 40 * 1024 * 1024
ID_BIG = 1e9
SQRT_HALF = 0.7071067811865476
ROW_TILE = 256
ADA_ROWS = 40

NT = (((1,), (1,)), ((), ()))
TN = (((0,), (0,)), ((), ()))
BF = jnp.bfloat16
F32 = jnp.float32


def _full(shape):
    return pl.BlockSpec(shape, lambda *_: (0,) * len(shape))


def _mm(a, b):
    return jnp.dot(a.astype(BF), b.astype(BF), preferred_element_type=F32)


def _rope_tables(n_tok, head_dim, n_heads):
    n_rows = n_tok // GRID_W
    rows = jnp.broadcast_to(jnp.arange(n_rows, dtype=F32)[:, None], (n_rows, GRID_W)).reshape(-1)
    cols = jnp.broadcast_to(jnp.arange(GRID_W, dtype=F32)[None, :], (n_rows, GRID_W)).reshape(-1)
    n_freq = head_dim // 4
    inv_freq = ROPE_BASE ** (-jnp.arange(n_freq, dtype=F32) / n_freq)
    ar, ac = rows[:, None] * inv_freq, cols[:, None] * inv_freq
    cos = jnp.concatenate([jnp.cos(ar), jnp.cos(ar), jnp.cos(ac), jnp.cos(ac)], axis=-1)
    sin = jnp.concatenate([-jnp.sin(ar), jnp.sin(ar), -jnp.sin(ac), jnp.sin(ac)], axis=-1)
    return jnp.tile(cos, (1, n_heads)), jnp.tile(sin, (1, n_heads))


def _swap_columns(n_heads, head_dim):
    q = head_dim // 4
    base = np.concatenate([np.arange(q, 2 * q), np.arange(0, q), np.arange(3 * q, 4 * q), np.arange(2 * q, 3 * q)])
    return (np.arange(n_heads)[:, None] * head_dim + base[None, :]).reshape(-1)


def _ada_kernel(c_ref, w_ref, b_ref, o_ref):
    c = c_ref[...]
    s = c / (1.0 + jnp.exp(-c))
    o_ref[...] = _mm(s, w_ref[...]) + b_ref[...]


def _ada_modulation(c_rows, w, b):
    d, f = w.shape
    tn = f // 4
    return pl.pallas_call(
        _ada_kernel,
        grid=(f // tn,),
        in_specs=[_full((ADA_ROWS, d)), pl.BlockSpec((d, tn), lambda j: (0, j)), pl.BlockSpec((1, tn), lambda j: (0, j))],
        out_specs=pl.BlockSpec((ADA_ROWS, tn), lambda j: (0, j)),
        out_shape=jax.ShapeDtypeStruct((ADA_ROWS, f), F32),
    )(c_rows, w.astype(BF), b.reshape(1, f))


def _norm_mod(x, gain, shift, scale):
    n = x * lax.rsqrt(jnp.mean(x * x, axis=-1, keepdims=True) + EPS) * gain
    return n * (1.0 + scale) + shift


def _proj_kernel(*refs, has_add, emit_h):
    refs = list(refs)
    h_ref = refs.pop(0)
    x = h_ref[...]
    if has_add:
        y_ref, gt_ref = refs.pop(0), refs.pop(0)
        x = x + gt_ref[0] * y_ref[...]
    g_ref, sh_ref, sc_ref, w_ref = refs[:4]
    outs = refs[4:]
    if emit_h:
        outs.pop(0)[...] = x
    a = _norm_mod(x, g_ref[...], sh_ref[0], sc_ref[0])
    outs[0][...] = _mm(a, w_ref[...])


def _in_projection(h, gain, shift, scale, w, rows_per_mod, add=None, emit_h=False):
    n, d = h.shape
    f = w.shape[1]
    tm = ROW_TILE
    rows = pl.BlockSpec((tm, d), lambda i: (i, 0))
    mod = pl.BlockSpec((1, 1, d), lambda i: ((i * tm) // rows_per_mod, 0, 0))
    args, specs = [h], [rows]
    if add is not None:
        args += [add[0], add[1]]
        specs += [rows, mod]
    args += [gain.reshape(1, d), shift, scale, w.astype(BF)]
    specs += [_full((1, d)), mod, mod, _full((d, f))]
    out_shape = [jax.ShapeDtypeStruct((n, f), F32)]
    out_specs = [pl.BlockSpec((tm, f), lambda i: (i, 0))]
    if emit_h:
        out_shape.insert(0, jax.ShapeDtypeStruct((n, d), F32))
        out_specs.insert(0, rows)
    res = pl.pallas_call(
        functools.partial(_proj_kernel, has_add=add is not None, emit_h=emit_h),
        grid=(n // tm,), in_specs=specs, out_specs=out_specs, out_shape=out_shape,
        compiler_params=pltpu.CompilerParams(vmem_limit_bytes=VMEM_MIXER_LIMIT),
    )(*args)
    return res if emit_h else res[0]


def _retention_kernel(logf_ref, logb_ref, ql_ref, kl_ref, vl_ref, qs_ref, ks_ref, qc_ref, kc_ref, vc_ref,
                      cos_ref, sin_ref, ol_ref, oc_ref):
    c = RET_CHUNK
    n_lat = ql_ref.shape[0] // c
    n_ctx = qc_ref.shape[0] // c
    k_scale = RET_QK_DIM ** -0.5
    row = lax.broadcasted_iota(jnp.int32, (c, c), 0).astype(F32)
    col = lax.broadcasted_iota(jnp.int32, (c, c), 1).astype(F32)
    rowk = lax.broadcasted_iota(jnp.int32, (c, RET_QK_DIM), 0).astype(F32)

    def step(qh, kh, vh, state, dmat, xi, zeta, cdm):
        qb, vb = qh.astype(BF), vh.astype(BF)
        scores = lax.dot_general(qb, kh.astype(BF), NT, preferred_element_type=F32) * dmat
        out = _mm(scores, vb) + _mm(qb, state) * xi
        kv = lax.dot_general((kh * zeta).astype(BF), vb, TN, preferred_element_type=F32)
        return out, cdm * state + kv

    for backward in (False, True):
        for h in range(RET_HEADS):
            lg = (logb_ref if backward else logf_ref)[h]
            if backward:
                dmat = jnp.where(col >= row, jnp.exp(lg * (col - row)), 0.0)
                xi = jnp.exp(lg * (c - row))
                zeta = jnp.exp(lg * rowk)
            else:
                dmat = jnp.where(row >= col, jnp.exp(lg * (row - col)), 0.0)
                xi = jnp.exp(lg * (row + 1.0))
                zeta = jnp.exp(lg * (c - 1.0 - rowk))
            cdm = jnp.exp(jnp.full((RET_QK_DIM, RET_V_DIM), lg * c, F32))
            qk = slice(h * RET_QK_DIM, (h + 1) * RET_QK_DIM)
            vv = slice(h * RET_V_DIM, (h + 1) * RET_V_DIM)
            state = jnp.zeros((RET_QK_DIM, RET_V_DIM), F32)
            for n in (range(n_ctx - 1, -1, -1) if backward else range(n_ctx)):
                r = slice(n * c, (n + 1) * c)
                out, state = step(qc_ref[r, qk], kc_ref[r, qk] * k_scale, vc_ref[r, vv], state, dmat, xi, zeta, cdm)
                oc_ref[r, vv] = oc_ref[r, vv] + out if backward else out

            def lat_chunk(i, state):
                n = (n_lat - 1 - i) if backward else i
                r = pl.ds(pl.multiple_of(n * c, c), c)
                cs, sn = cos_ref[r, qk], sin_ref[r, qk]
                qh = ql_ref[r, qk] * cs + qs_ref[r, qk] * sn
                kh = (kl_ref[r, qk] * cs + ks_ref[r, qk] * sn) * k_scale
                out, state = step(qh, kh, vl_ref[r, vv], state, dmat, xi, zeta, cdm)
                ol_ref[r, vv] = ol_ref[r, vv] + out if backward else out
                return state

            lax.fori_loop(0, n_lat, lat_chunk, state)


def _retention(p_lat, p_ctx, log_f, log_b, cos, sin, batch):
    s = p_lat.shape[0] // batch
    c = p_ctx.shape[0] // batch
    qw = RET_Q_W
    swap0 = EV_COLS // qw
    smem = pl.BlockSpec(memory_space=pltpu.SMEM)
    lat = lambda width, j: pl.BlockSpec((s, width), lambda b: (b, j))
    ctx = lambda width, j: pl.BlockSpec((c, width), lambda b: (b, j))
    return pl.pallas_call(
        _retention_kernel,
        grid=(batch,),
        in_specs=[smem, smem, lat(qw, 0), lat(qw, 1), lat(RET_V_W, 1), lat(qw, swap0), lat(qw, swap0 + 1),
                  ctx(qw, 0), ctx(qw, 1), ctx(RET_V_W, 1), _full((s, qw)), _full((s, qw))],
        out_specs=[lat(RET_V_W, 0), ctx(RET_V_W, 0)],
        out_shape=[jax.ShapeDtypeStruct((batch * s, RET_V_W), F32), jax.ShapeDtypeStruct((batch * c, RET_V_W), F32)],
        compiler_params=pltpu.CompilerParams(vmem_limit_bytes=VMEM_MIXER_LIMIT),
    )(log_f, log_b, p_lat, p_lat, p_lat, p_lat, p_lat, p_ctx, p_ctx, p_ctx, cos, sin)


def _even_out_kernel(h_ref, gate_ref, ret_ref, g_ref, gb_ref, gc_ref, x_ref, gcp_ref, xp_ref, gcn_ref, xn_ref,
                     cw_ref, w_ref, o_ref, *, seq_blocks):
    i = pl.program_id(0)
    tm = h_ref.shape[0]
    u = gc_ref[...] * x_ref[...]
    seq_pos = i % seq_blocks
    keep_prev = jnp.where(seq_pos == 0, 0.0, 1.0)
    keep_next = jnp.where(seq_pos == seq_blocks - 1, 0.0, 1.0)
    u_prev = gcp_ref[7:8, :] * xp_ref[7:8, :] * keep_prev
    u_next = gcn_ref[0:1, :] * xn_ref[0:1, :] * keep_next
    rid = lax.broadcasted_iota(jnp.int32, u.shape, 0)
    up = jnp.where(rid == 0, u_prev, pltpu.roll(u, 1, 0))
    un = jnp.where(rid == tm - 1, u_next, pltpu.roll(u, tm - 1, 0))
    conv = cw_ref[0:1, :] * up + cw_ref[1:2, :] * u + cw_ref[2:3, :] * un
    parts = []
    for hh in range(RET_HEADS):
        lanes = slice(hh * RET_V_DIM, (hh + 1) * RET_V_DIM)
        r = ret_ref[:, lanes]
        g = g_ref[:, lanes]
        parts.append(r * lax.rsqrt(jnp.mean(r * r, axis=-1, keepdims=True) + EPS) * (g / (1.0 + jnp.exp(-g))))
    y = jnp.concatenate(parts + [gb_ref[...] * conv], axis=1)
    o_ref[...] = h_ref[...] + gate_ref[0] * _mm(y, w_ref[...])


def _even_output(h, gate, ret, p, conv_w, w_out, seq_len, rows_per_mod):
    n, d = h.shape
    tm = ROW_TILE
    cw = CONV_CH
    tiles = tm // 8
    last_tile = n // 8 - 1
    rows = lambda width, j: pl.BlockSpec((tm, width), lambda i: (i, j))
    prev = lambda j: pl.BlockSpec((8, cw), lambda i: (jnp.maximum(i * tiles - 1, 0), j))
    nxt = lambda j: pl.BlockSpec((8, cw), lambda i: (jnp.minimum((i + 1) * tiles, last_tile), j))
    mod = pl.BlockSpec((1, 1, d), lambda i: ((i * tm) // rows_per_mod, 0, 0))
    return pl.pallas_call(
        functools.partial(_even_out_kernel, seq_blocks=seq_len // tm),
        grid=(n // tm,),
        in_specs=[rows(d, 0), mod, rows(RET_V_W, 0), rows(cw, 2), rows(cw, 3), rows(cw, 4), rows(cw, 5),
                  prev(4), prev(5), nxt(4), nxt(5), _full((CONV_K, cw)), _full((d, d))],
        out_specs=rows(d, 0),
        out_shape=jax.ShapeDtypeStruct((n, d), F32),
    )(h, gate, ret, p, p, p, p, p, p, p, p, conv_w, w_out.astype(BF))


def _attn_kernel(sink_ref, h_ref, gate_ref, q_ref, qs_ref, kp_ref, kc_ref, kn_ref, ksp_ref, ksc_ref, ksn_ref,
                 vp_ref, vc_ref, vn_ref, kx_ref, vx_ref, cq_ref, sq_ref, ckp_ref, ckc_ref, ckn_ref,
                 skp_ref, skc_ref, skn_ref, w_ref, o_ref, *, n_lat):
    j = pl.program_id(1)
    blk = ATT_BLOCK
    span = blk + 2 * WINDOW
    n_keys = span + kx_ref.shape[0]
    scale = ATT_HEAD_DIM ** -0.5
    q = (q_ref[...] * cq_ref[...] + qs_ref[...] * sq_ref[...]) * scale
    keys = jnp.concatenate([kp_ref[...] * ckp_ref[...] + ksp_ref[...] * skp_ref[...],
                            kc_ref[...] * ckc_ref[...] + ksc_ref[...] * skc_ref[...],
                            kn_ref[...] * ckn_ref[...] + ksn_ref[...] * skn_ref[...],
                            kx_ref[...]], axis=0).astype(BF)
    vals = jnp.concatenate([vp_ref[...], vc_ref[...], vn_ref[...], vx_ref[...]], axis=0).astype(BF)
    qpos = lax.broadcasted_iota(jnp.int32, (blk, n_keys), 0)
    r = lax.broadcasted_iota(jnp.int32, (blk, n_keys), 1)
    key_pos = (j - 1) * blk + r
    in_band = (r >= qpos) & (r <= qpos + 2 * WINDOW) & (key_pos >= 0) & (key_pos < n_lat)
    valid = jnp.concatenate([in_band | (r >= span)] * ATT_GROUP, axis=0)
    heads = [None] * ATT_HEADS
    for kh in range(ATT_KV_HEADS):
        kv = slice(kh * ATT_HEAD_DIM, (kh + 1) * ATT_HEAD_DIM)
        qg = jnp.concatenate([q[:, (kh * ATT_GROUP + g) * ATT_HEAD_DIM:(kh * ATT_GROUP + g + 1) * ATT_HEAD_DIM]
                              for g in range(ATT_GROUP)], axis=0)
        s = lax.dot_general(qg.astype(BF), keys[:, kv], NT, preferred_element_type=F32)
        s = jnp.where(valid, s, NEG_INF)
        sink = jnp.concatenate([jnp.full((blk, 1), sink_ref[kh * ATT_GROUP + g], F32) for g in range(ATT_GROUP)], axis=0)
        m = jnp.maximum(jnp.max(s, axis=-1, keepdims=True), sink)
        p = jnp.exp(s - m)
        den = jnp.sum(p, axis=-1, keepdims=True) + jnp.exp(sink - m)
        o = _mm(p, vals[:, kv]) / den
        for g in range(ATT_GROUP):
            heads[kh * ATT_GROUP + g] = o[g * blk:(g + 1) * blk]
    o_ref[...] = h_ref[...] + gate_ref[0] * _mm(jnp.concatenate(heads, axis=1), w_ref[...])


def _attention(h, gate, p_lat, p_ctx, sinks, cos, sin, w_out, batch):
    n, d = h.shape
    s = n // batch
    c = p_ctx.shape[0] // batch
    blk = ATT_BLOCK
    nb = s // blk
    kw = ATT_KV_W
    k0 = 2 * ATT_Q_W // kw
    row = lambda width, col: pl.BlockSpec((blk, width), lambda b, j: (b * nb + j, col))
    prv = lambda width, col: pl.BlockSpec((blk, width), lambda b, j: (b * nb + jnp.maximum(j - 1, 0), col))
    nxt = lambda width, col: pl.BlockSpec((blk, width), lambda b, j: (b * nb + jnp.minimum(j + 1, nb - 1), col))
    tab = lambda width: pl.BlockSpec((blk, width), lambda b, j: (j, 0))
    tab_p = lambda width: pl.BlockSpec((blk, width), lambda b, j: (jnp.maximum(j - 1, 0), 0))
    tab_n = lambda width: pl.BlockSpec((blk, width), lambda b, j: (jnp.minimum(j + 1, nb - 1), 0))
    ctx = lambda col: pl.BlockSpec((c, kw), lambda b, j: (b, col))
    mod = pl.BlockSpec((1, 1, d), lambda b, j: (b, 0, 0))
    return pl.pallas_call(
        functools.partial(_attn_kernel, n_lat=s),
        grid=(batch, nb),
        in_specs=[pl.BlockSpec(memory_space=pltpu.SMEM), row(d, 0), mod, row(ATT_Q_W, 0), row(ATT_Q_W, 1),
                  prv(kw, k0), row(kw, k0), nxt(kw, k0), prv(kw, k0 + 1), row(kw, k0 + 1), nxt(kw, k0 + 1),
                  prv(kw, k0 + 2), row(kw, k0 + 2), nxt(kw, k0 + 2), ctx(0), ctx(1),
                  tab(ATT_Q_W), tab(ATT_Q_W), tab_p(kw), tab(kw), tab_n(kw), tab_p(kw), tab(kw), tab_n(kw),
                  pl.BlockSpec((d, d), lambda b, j: (0, 0))],
        out_specs=row(d, 0),
        out_shape=jax.ShapeDtypeStruct((n, d), F32),
    )(sinks, h, gate, p_lat, p_lat, p_lat, p_lat, p_lat, p_lat, p_lat, p_lat, p_lat, p_lat, p_lat, p_ctx, p_ctx,
      cos, sin, cos, cos, cos, sin, sin, sin, w_out.astype(BF))


def _pack_bf16_table(tab):
    e = tab.shape[0]
    bits = lax.bitcast_convert_type(tab.astype(BF), jnp.uint16).astype(jnp.uint32)
    word = (bits[:, HALF_D:] << 16) | bits[:, :HALF_D]
    return lax.bitcast_convert_type(word, jnp.int32).reshape(e, SLAB, 128)


def _unpack_words(words):
    lo = lax.bitcast_convert_type(words << 16, F32)
    hi = lax.bitcast_convert_type(words & BF16_HI_MASK, F32)
    return lo, hi


def _extract_top(s, order, n):
    vals, ids = [], []
    for _ in range(n):
        m = jnp.max(s, axis=0, keepdims=True)
        first = jnp.min(jnp.where(s == m, order, ID_BIG), axis=0, keepdims=True)
        vals.append(m)
        ids.append(first)
        s = jnp.where(order == first, -jnp.inf, s)
    return vals, ids


def _route_kernel(h_ref, g_ref, sh_ref, sc_ref, wq_ref, k1_ref, k2_ref, f_ref, idx_ref, gate_ref,
                  v1_ref, v2_ref, i1_ref, i2_ref, et_ref, gt_ref):
    tb = h_ref.shape[0]
    f = _norm_mod(h_ref[...], g_ref[...], sh_ref[0], sc_ref[0])
    f_ref[...] = f
    qb = _mm(f, wq_ref[...]).astype(BF)
    key_id = lax.broadcasted_iota(jnp.int32, (PEER_N_KEYS, tb), 0).astype(F32)
    r8 = lax.broadcasted_iota(jnp.int32, (8, tb), 0).astype(F32)
    flat = jnp.concatenate([r8 * 16, (r8 + 8) * 16, r8 * 16 + 1, r8 + 8, r8, r8 + 16, r8 + 32, r8 + 48, r8 + 64], axis=0)
    for h in range(PEER_HEADS):
        qh = qb[:, h * PEER_D_KEY:(h + 1) * PEER_D_KEY]
        s1 = lax.dot_general(k1_ref[h], qh, NT, preferred_element_type=F32)
        s2 = lax.dot_general(k2_ref[h], qh, NT, preferred_element_type=F32)
        for s, v_ref, i_ref in ((s1, v1_ref, i1_ref), (s2, v2_ref, i2_ref)):
            vals, ids = _extract_top(s, key_id, PEER_TOPK)
            for k in range(PEER_TOPK):
                v_ref[k:k + 1, :] = vals[k]
                i_ref[k:k + 1, :] = ids[k]
        v1a, v1b, v2a, v2b = v1_ref[0:8, :], v1_ref[8:16, :], v2_ref[0:8, :], v2_ref[8:16, :]
        i1a, i1b, i2a, i2b = i1_ref[0:8, :], i1_ref[8:16, :], i2_ref[0:8, :], i2_ref[8:16, :]
        ninf = -jnp.inf
        cand = jnp.concatenate([
            v1a + v2a[0:1], v1b + v2a[0:1], v1a + v2a[1:2], v2b + v1a[0:1],
            jnp.where(r8 >= 2, v2a + v1a[0:1], ninf),
            jnp.where(r8 >= 2, v2a + v1a[1:2], ninf),
            jnp.where((r8 >= 2) & (r8 <= 4), v2a + v1a[2:3], ninf),
            jnp.where((r8 >= 2) & (r8 <= 3), v2a + v1a[3:4], ninf),
            jnp.where(r8 == 2, v2a + v1a[4:5], ninf)], axis=0)
        expert = jnp.concatenate([
            i1a * PEER_N_KEYS + i2a[0:1], i1b * PEER_N_KEYS + i2a[0:1], i1a * PEER_N_KEYS + i2a[1:2],
            i1a[0:1] * PEER_N_KEYS + i2b,
            i1a[0:1] * PEER_N_KEYS + i2a, i1a[1:2] * PEER_N_KEYS + i2a, i1a[2:3] * PEER_N_KEYS + i2a,
            i1a[3:4] * PEER_N_KEYS + i2a, i1a[4:5] * PEER_N_KEYS + i2a], axis=0)
        cs, picks = _extract_top(cand, flat, PEER_TOPK)
        ex = [jnp.exp(c - cs[0]) for c in cs]
        den = ex[0]
        for e in ex[1:]:
            den = den + e
        for k in range(PEER_TOPK):
            row = h * PEER_TOPK + k
            pick = jnp.max(jnp.where(flat == picks[k], expert, -1.0), axis=0, keepdims=True)
            et_ref[row:row + 1, :] = pick.astype(jnp.int32)
            gt_ref[row:row + 1, :] = ex[k] / den
    idx_ref[...] = et_ref[...].T
    gate_ref[...] = gt_ref[...].T


def _peer_route(h, gain, shift, scale, rows_per_mod, wq, k1, k2):
    t, d = h.shape
    tb = ROUTE_TOKENS
    mod = pl.BlockSpec((1, 1, d), lambda i: ((i * tb) // rows_per_mod, 0, 0))
    slots = pl.BlockSpec((tb, N_SLOTS), lambda i: (i, 0))
    return pl.pallas_call(
        _route_kernel,
        grid=(t // tb,),
        in_specs=[pl.BlockSpec((tb, d), lambda i: (i, 0)), _full((1, d)), mod, mod,
                  _full((d, PEER_HEADS * PEER_D_KEY)), _full(k1.shape), _full(k2.shape)],
        out_specs=[pl.BlockSpec((tb, d), lambda i: (i, 0)), slots, slots],
        out_shape=[jax.ShapeDtypeStruct((t, d), F32), jax.ShapeDtypeStruct((t, N_SLOTS), jnp.int32),
                   jax.ShapeDtypeStruct((t, N_SLOTS), F32)],
        scratch_shapes=[pltpu.VMEM((PEER_TOPK, tb), F32), pltpu.VMEM((PEER_TOPK, tb), F32),
                        pltpu.VMEM((PEER_TOPK, tb), F32), pltpu.VMEM((PEER_TOPK, tb), F32),
                        pltpu.VMEM((N_SLOTS, tb), jnp.int32), pltpu.VMEM((N_SLOTS, tb), F32)],
    )(h, gain.reshape(1, d), shift, scale, wq, k1, k2)


def _peer_u_kernel(idx_ref, x_ref, gate_ref, tab_ref, w_ref, p_ref, r_ref):
    tb = x_ref.shape[0]
    rows = N_SLOTS * SLAB

    for t in range(tb):
        x = x_ref[t]
        xlo, xhi = x[0:SLAB], x[SLAB:2 * SLAB]
        slot_idx = idx_ref.at[t]
        for k in range(N_SLOTS):
            lo, hi = _unpack_words(tab_ref[slot_idx[k]])
            p_ref[pl.ds(t * rows + k * SLAB, SLAB), :] = lo * xlo + hi * xhi

    ones = jnp.ones((8, 128), BF)
    for t in range(tb):
        part = p_ref[pl.ds(t * rows, N_SLOTS, stride=SLAB), :]
        for s in range(1, SLAB):
            part = part + p_ref[pl.ds(t * rows + s, N_SLOTS, stride=SLAB), :]
        hi = part.astype(BF)
        lo = (part - hi.astype(F32)).astype(BF)
        r_ref[t:t + 1, :] = (lax.dot_general(ones, hi, NT, preferred_element_type=F32)
                             + lax.dot_general(ones, lo, NT, preferred_element_type=F32))[0:1]
    r = r_ref[...]
    w_ref[...] = 0.5 * r * (1.0 + lax.erf(r * SQRT_HALF)) * gate_ref[...]


def _peer_v_kernel(idx_ref, w_ref, tab_ref, y_ref, wrep_ref):
    tb = y_ref.shape[0]
    n_acc = 4
    for t in range(tb):
        wrep_ref[t] = jnp.broadcast_to(w_ref[t:t + 1, :], (N_SLOTS, 128)).T

    for t in range(tb):
        acc_lo = [jnp.zeros((SLAB, 128), F32) for _ in range(n_acc)]
        acc_hi = [jnp.zeros((SLAB, 128), F32) for _ in range(n_acc)]
        slot_idx = idx_ref.at[t]
        for k in range(N_SLOTS):
            lo, hi = _unpack_words(tab_ref[slot_idx[k]])
            w = wrep_ref[t, pl.ds(k, 1), :]
            acc_lo[k % n_acc] = acc_lo[k % n_acc] + w * lo
            acc_hi[k % n_acc] = acc_hi[k % n_acc] + w * hi
        y_ref[t, 0:SLAB, :] = (acc_lo[0] + acc_lo[1]) + (acc_lo[2] + acc_lo[3])
        y_ref[t, SLAB:2 * SLAB, :] = (acc_hi[0] + acc_hi[1]) + (acc_hi[2] + acc_hi[3])


def _peer_experts(f, idx, gate, u_words, v_words):
    t, d = f.shape
    tb = PEER_TOKENS
    smem = pl.BlockSpec((tb, N_SLOTS), lambda i: (i, 0), memory_space=pltpu.SMEM)
    slots = pl.BlockSpec((tb, N_SLOTS), lambda i: (i, 0))
    resident = pl.BlockSpec(memory_space=pltpu.VMEM)
    rows3 = pl.BlockSpec((tb, 2 * SLAB, 128), lambda i: (i, 0, 0))
    params = pltpu.CompilerParams(vmem_limit_bytes=VMEM_TABLE_LIMIT)
    w = pl.pallas_call(
        _peer_u_kernel,
        grid=(t // tb,),
        in_specs=[smem, rows3, slots, resident],
        out_specs=slots,
        out_shape=jax.ShapeDtypeStruct((t, N_SLOTS), F32),
        scratch_shapes=[pltpu.VMEM((tb * N_SLOTS * SLAB, 128), F32), pltpu.VMEM((tb, N_SLOTS), F32)],
        compiler_params=params,
    )(idx, f.reshape(t, 2 * SLAB, 128), gate, u_words)
    y = pl.pallas_call(
        _peer_v_kernel,
        grid=(t // tb,),
        in_specs=[smem, slots, resident],
        out_specs=rows3,
        out_shape=jax.ShapeDtypeStruct((t, 2 * SLAB, 128), F32),
        scratch_shapes=[pltpu.VMEM((tb, N_SLOTS, 128), F32)],
        compiler_params=params,
    )(idx, w, v_words)
    return y.reshape(t, d)


class _PeerWeights:
    def __init__(self, wq, keys1, keys2, u_tab, v_tab):
        half = PEER_D_KEY // 2
        self.wq = wq.astype(BF)
        self.k1 = jnp.pad(keys1, ((0, 0), (0, 0), (0, half))).astype(BF)
        self.k2 = jnp.pad(keys2, ((0, 0), (0, 0), (half, 0))).astype(BF)
        self.u = _pack_bf16_table(u_tab)
        self.v = _pack_bf16_table(v_tab)


def _peer_ffn(h, gain, shift, scale, rows_per_mod, pw):
    f, idx, gate = _peer_route(h, gain, shift, scale, rows_per_mod, pw.wq, pw.k1, pw.k2)
    return _peer_experts(f, idx, gate, pw.u, pw.v)


def _final_kernel(h_ref, y_ref, gate_ref, g_ref, o_ref):
    x = h_ref[...] + gate_ref[0] * y_ref[...]
    o_ref[...] = x * lax.rsqrt(jnp.mean(x * x, axis=-1, keepdims=True) + EPS) * g_ref[...]


def _final_norm(h, y, gate, gain, rows_per_mod):
    n, d = h.shape
    tm = ROW_TILE
    rows = pl.BlockSpec((tm, d), lambda i: (i, 0))
    mod = pl.BlockSpec((1, 1, d), lambda i: ((i * tm) // rows_per_mod, 0, 0))
    return pl.pallas_call(
        _final_kernel, grid=(n // tm,), in_specs=[rows, rows, mod, _full((1, d))], out_specs=rows,
        out_shape=jax.ShapeDtypeStruct((n, d), F32),
    )(h, y, gate, gain.reshape(1, d))


def kernel(x, c, ctx, c_ctx, ada_w, ada_b, mix_norm_g, ffn_norm_g, ev_w_in, ev_w_out,
           ret_decay_logit_f, ret_decay_logit_b, conv_w, od_w_in, od_w_out, attn_sinks,
           peer_wq, peer_keys1, peer_keys2, peer_u, peer_v, final_norm_g):
    batch, s, d = x.shape
    n_ctx = ctx.shape[1]
    assert DEPTH == 2 and s % ROW_TILE == 0 and n_ctx % ROW_TILE == 0 and batch + 1 <= ADA_ROWS
    h_lat = x.reshape(batch * s, d)
    h_ctx = ctx.reshape(batch * n_ctx, d)
    c_rows = jnp.zeros((ADA_ROWS, d), F32).at[:batch].set(c).at[batch].set(c_ctx)

    def modulation(layer):
        mod = _ada_modulation(c_rows, ada_w[layer], ada_b[layer])
        lat = [m.reshape(batch, 1, d) for m in jnp.split(mod[:batch], N_ADA, axis=-1)]
        cx = [m.reshape(1, 1, d) for m in jnp.split(mod[batch:batch + 1], N_ADA, axis=-1)]
        return lat, cx

    (sh1, sc1, g1, sh2, sc2, g2), (csh1, csc1, cg1, csh2, csc2, cg2) = modulation(0)
    w_in = ev_w_in[0]
    swap = _swap_columns(RET_HEADS, RET_QK_DIM)
    w0 = jnp.concatenate([w_in, w_in[:, :RET_Q_W][:, swap], w_in[:, RET_Q_W:2 * RET_Q_W][:, swap]], axis=1)
    p_lat = _in_projection(h_lat, mix_norm_g[0], sh1, sc1, w0, s)
    p_ctx = _in_projection(h_ctx, mix_norm_g[0], csh1, csc1, w0, batch * n_ctx)
    log_f = jax.nn.log_sigmoid(ret_decay_logit_f[0].astype(F32))
    log_b = jax.nn.log_sigmoid(ret_decay_logit_b[0].astype(F32))
    cos, sin = _rope_tables(s, RET_QK_DIM, RET_HEADS)
    ret_lat, ret_ctx = _retention(p_lat, p_ctx, log_f, log_b, cos, sin, batch)
    h_lat = _even_output(h_lat, g1, ret_lat, p_lat, conv_w[0], ev_w_out[0], s, s)
    h_ctx = _even_output(h_ctx, cg1, ret_ctx, p_ctx, conv_w[0], ev_w_out[0], n_ctx, batch * n_ctx)
    pw = _PeerWeights(peer_wq[0], peer_keys1[0], peer_keys2[0], peer_u[0], peer_v[0])
    y_lat = _peer_ffn(h_lat, ffn_norm_g[0], sh2, sc2, s, pw)
    y_ctx = _peer_ffn(h_ctx, ffn_norm_g[0], csh2, csc2, batch * n_ctx, pw)

    (sh1, sc1, g1, sh2, sc2, g2b), (csh1, csc1, _, _, _, _) = modulation(1)
    w_in = od_w_in[0]
    wq_cols, wk_cols, wv_cols = w_in[:, :ATT_Q_W], w_in[:, ATT_Q_W:ATT_Q_W + ATT_KV_W], w_in[:, ATT_Q_W + ATT_KV_W:]
    w1 = jnp.concatenate([wq_cols, wq_cols[:, _swap_columns(ATT_HEADS, ATT_HEAD_DIM)], wk_cols,
                          wk_cols[:, _swap_columns(ATT_KV_HEADS, ATT_HEAD_DIM)], wv_cols], axis=1)
    h_lat, p_lat = _in_projection(h_lat, mix_norm_g[1], sh1, sc1, w1, s, add=(y_lat, g2), emit_h=True)
    p_ctx = _in_projection(h_ctx, mix_norm_g[1], csh1, csc1, w_in[:, ATT_Q_W:], batch * n_ctx, add=(y_ctx, cg2))
    cos, sin = _rope_tables(s, ATT_HEAD_DIM, ATT_HEADS)
    h_lat = _attention(h_lat, g1, p_lat, p_ctx, attn_sinks[0].astype(F32), cos, sin, od_w_out[0], batch)
    pw = _PeerWeights(peer_wq[1], peer_keys1[1], peer_keys2[1], peer_u[1], peer_v[1])
    y_lat = _peer_ffn(h_lat, ffn_norm_g[1], sh2, sc2, s, pw)
    out = _final_norm(h_lat, y_lat, g2b, final_norm_g, s)
    return out.reshape(batch, s, d)
```

```python
import functools

import numpy as np
import jax
import jax.numpy as jnp
from jax import lax
from jax.experimental import pallas as pl
from jax.experimental.pallas import tpu as pltpu

D_MODEL = 1024
DEPTH = 2
GRID_W = 64
EPS = 1e-6
ROPE_BASE = 10000.0
NEG_INF = -1e30
N_ADA = 6
RET_HEADS = 4
RET_V_DIM = D_MODEL // (2 * RET_HEADS)
RET_QK_DIM = RET_V_DIM // 2
RET_CHUNK = 128
CONV_CH = D_MODEL // 2
CONV_K = 3
RET_Q_W = RET_HEADS * RET_QK_DIM
RET_V_W = RET_HEADS * RET_V_DIM
EV_COLS = 2 * RET_Q_W + 2 * RET_V_W + 3 * CONV_CH
ATT_HEADS = 16
ATT_HEAD_DIM = D_MODEL // ATT_HEADS
ATT_KV_HEADS = 4
ATT_GROUP = ATT_HEADS // ATT_KV_HEADS
WINDOW = 128
ATT_BLOCK = 128
ATT_Q_W = ATT_HEADS * ATT_HEAD_DIM
ATT_KV_W = ATT_KV_HEADS * ATT_HEAD_DIM
PEER_HEADS = 8
PEER_N_KEYS = 128
PEER_D_KEY = 128
PEER_TOPK = 16
ROUTE_TOKENS = 128
PEER_TOKENS = 32
N_SLOTS = PEER_HEADS * PEER_TOPK
HALF_D = D_MODEL // 2
SLAB = HALF_D // 128
PAIR = 8 // SLAB
BF16_HI_MASK = -65536
VMEM_TABLE_LIMIT = 48 * 1024 * 1024
VMEM_MIXER_LIMIT = 40 * 1024 * 1024
ID_BIG = 1e9
SQRT_HALF = 0.7071067811865476
ROW_TILE = 256
ADA_ROWS = 40

NT = (((1,), (1,)), ((), ()))
TN = (((0,), (0,)), ((), ()))
BF = jnp.bfloat16
F32 = jnp.float32


def _full(shape):
    return pl.BlockSpec(shape, lambda *_: (0,) * len(shape))


def _mm(a, b):
    return jnp.dot(a.astype(BF), b.astype(BF), preferred_element_type=F32)


def _rope_tables(n_tok, head_dim, n_heads):
    n_rows = n_tok // GRID_W
    rows = jnp.broadcast_to(jnp.arange(n_rows, dtype=F32)[:, None], (n_rows, GRID_W)).reshape(-1)
    cols = jnp.broadcast_to(jnp.arange(GRID_W, dtype=F32)[None, :], (n_rows, GRID_W)).reshape(-1)
    n_freq = head_dim // 4
    inv_freq = ROPE_BASE ** (-jnp.arange(n_freq, dtype=F32) / n_freq)
    ar, ac = rows[:, None] * inv_freq, cols[:, None] * inv_freq
    cos = jnp.concatenate([jnp.cos(ar), jnp.cos(ar), jnp.cos(ac), jnp.cos(ac)], axis=-1)
    sin = jnp.concatenate([-jnp.sin(ar), jnp.sin(ar), -jnp.sin(ac), jnp.sin(ac)], axis=-1)
    return jnp.tile(cos, (1, n_heads)), jnp.tile(sin, (1, n_heads))


def _swap_columns(n_heads, head_dim):
    q = head_dim // 4
    base = np.concatenate([np.arange(q, 2 * q), np.arange(0, q), np.arange(3 * q, 4 * q), np.arange(2 * q, 3 * q)])
    return (np.arange(n_heads)[:, None] * head_dim + base[None, :]).reshape(-1)


def _ada_kernel(c_ref, w_ref, b_ref, o_ref):
    c = c_ref[...]
    s = c / (1.0 + jnp.exp(-c))
    o_ref[...] = _mm(s, w_ref[...]) + b_ref[...]


def _ada_modulation(c_rows, w, b):
    d, f = w.shape
    tn = f // 4
    return pl.pallas_call(
        _ada_kernel,
        grid=(f // tn,),
        in_specs=[_full((ADA_ROWS, d)), pl.BlockSpec((d, tn), lambda j: (0, j)), pl.BlockSpec((1, tn), lambda j: (0, j))],
        out_specs=pl.BlockSpec((ADA_ROWS, tn), lambda j: (0, j)),
        out_shape=jax.ShapeDtypeStruct((ADA_ROWS, f), F32),
    )(c_rows, w.astype(BF), b.reshape(1, f))


def _norm_mod(x, gain, shift, scale):
    n = x * lax.rsqrt(jnp.mean(x * x, axis=-1, keepdims=True) + EPS) * gain
    return n * (1.0 + scale) + shift


def _proj_kernel(*refs, has_add, emit_h):
    refs = list(refs)
    h_ref = refs.pop(0)
    x = h_ref[...]
    if has_add:
        y_ref, gt_ref = refs.pop(0), refs.pop(0)
        x = x + gt_ref[0] * y_ref[...]
    g_ref, sh_ref, sc_ref, w_ref = refs[:4]
    outs = refs[4:]
    if emit_h:
        outs.pop(0)[...] = x
    a = _norm_mod(x, g_ref[...], sh_ref[0], sc_ref[0])
    outs[0][...] = _mm(a, w_ref[...])


def _in_projection(h, gain, shift, scale, w, rows_per_mod, add=None, emit_h=False):
    n, d = h.shape
    f = w.shape[1]
    tm = ROW_TILE
    rows = pl.BlockSpec((tm, d), lambda i: (i, 0))
    mod = pl.BlockSpec((1, 1, d), lambda i: ((i * tm) // rows_per_mod, 0, 0))
    args, specs = [h], [rows]
    if add is not None:
        args += [add[0], add[1]]
        specs += [rows, mod]
    args += [gain.reshape(1, d), shift, scale, w.astype(BF)]
    specs += [_full((1, d)), mod, mod, _full((d, f))]
    out_shape = [jax.ShapeDtypeStruct((n, f), F32)]
    out_specs = [pl.BlockSpec((tm, f), lambda i: (i, 0))]
    if emit_h:
        out_shape.insert(0, jax.ShapeDtypeStruct((n, d), F32))
        out_specs.insert(0, rows)
    res = pl.pallas_call(
        functools.partial(_proj_kernel, has_add=add is not None, emit_h=emit_h),
        grid=(n // tm,), in_specs=specs, out_specs=out_specs, out_shape=out_shape,
        compiler_params=pltpu.CompilerParams(vmem_limit_bytes=VMEM_MIXER_LIMIT),
    )(*args)
    return res if emit_h else res[0]


def _retention_kernel(logf_ref, logb_ref, ql_ref, kl_ref, vl_ref, qs_ref, ks_ref, qc_ref, kc_ref, vc_ref,
                      cos_ref, sin_ref, ol_ref, oc_ref):
    c = RET_CHUNK
    n_lat = ql_ref.shape[0] // c
    n_ctx = qc_ref.shape[0] // c
    k_scale = RET_QK_DIM ** -0.5
    row = lax.broadcasted_iota(jnp.int32, (c, c), 0).astype(F32)
    col = lax.broadcasted_iota(jnp.int32, (c, c), 1).astype(F32)
    rowk = lax.broadcasted_iota(jnp.int32, (c, RET_QK_DIM), 0).astype(F32)

    def step(qh, kh, vh, state, dmat, xi, zeta, cdm):
        qb, vb = qh.astype(BF), vh.astype(BF)
        scores = lax.dot_general(qb, kh.astype(BF), NT, preferred_element_type=F32) * dmat
        out = _mm(scores, vb) + _mm(qb, state) * xi
        kv = lax.dot_general((kh * zeta).astype(BF), vb, TN, preferred_element_type=F32)
        return out, cdm * state + kv

    for backward in (False, True):
        for h in range(RET_HEADS):
            lg = (logb_ref if backward else logf_ref)[h]
            if backward:
                dmat = jnp.where(col >= row, jnp.exp(lg * (col - row)), 0.0)
                xi = jnp.exp(lg * (c - row))
                zeta = jnp.exp(lg * rowk)
            else:
                dmat = jnp.where(row >= col, jnp.exp(lg * (row - col)), 0.0)
                xi = jnp.exp(lg * (row + 1.0))
                zeta = jnp.exp(lg * (c - 1.0 - rowk))
            cdm = jnp.exp(jnp.full((RET_QK_DIM, RET_V_DIM), lg * c, F32))
            qk = slice(h * RET_QK_DIM, (h + 1) * RET_QK_DIM)
            vv = slice(h * RET_V_DIM, (h + 1) * RET_V_DIM)
            state = jnp.zeros((RET_QK_DIM, RET_V_DIM), F32)
            for n in (range(n_ctx - 1, -1, -1) if backward else range(n_ctx)):
                r = slice(n * c, (n + 1) * c)
                out, state = step(qc_ref[r, qk], kc_ref[r, qk] * k_scale, vc_ref[r, vv], state, dmat, xi, zeta, cdm)
                oc_ref[r, vv] = oc_ref[r, vv] + out if backward else out

            def lat_chunk(i, state):
                n = (n_lat - 1 - i) if backward else i
                r = pl.ds(pl.multiple_of(n * c, c), c)
                cs, sn = cos_ref[r, qk], sin_ref[r, qk]
                qh = ql_ref[r, qk] * cs + qs_ref[r, qk] * sn
                kh = (kl_ref[r, qk] * cs + ks_ref[r, qk] * sn) * k_scale
                out, state = step(qh, kh, vl_ref[r, vv], state, dmat, xi, zeta, cdm)
                ol_ref[r, vv] = ol_ref[r, vv] + out if backward else out
                return state

            lax.fori_loop(0, n_lat, lat_chunk, state)


def _retention(p_lat, p_ctx, log_f, log_b, cos, sin, batch):
    s = p_lat.shape[0] // batch
    c = p_ctx.shape[0] // batch
    qw = RET_Q_W
    swap0 = EV_COLS // qw
    smem = pl.BlockSpec(memory_space=pltpu.SMEM)
    lat = lambda width, j: pl.BlockSpec((s, width), lambda b: (b, j))
    ctx = lambda width, j: pl.BlockSpec((c, width), lambda b: (b, j))
    return pl.pallas_call(
        _retention_kernel,
        grid=(batch,),
        in_specs=[smem, smem, lat(qw, 0), lat(qw, 1), lat(RET_V_W, 1), lat(qw, swap0), lat(qw, swap0 + 1),
                  ctx(qw, 0), ctx(qw, 1), ctx(RET_V_W, 1), _full((s, qw)), _full((s, qw))],
        out_specs=[lat(RET_V_W, 0), ctx(RET_V_W, 0)],
        out_shape=[jax.ShapeDtypeStruct((batch * s, RET_V_W), F32), jax.ShapeDtypeStruct((batch * c, RET_V_W), F32)],
        compiler_params=pltpu.CompilerParams(vmem_limit_bytes=VMEM_MIXER_LIMIT),
    )(log_f, log_b, p_lat, p_lat, p_lat, p_lat, p_lat, p_ctx, p_ctx, p_ctx, cos, sin)


def _even_out_kernel(h_ref, gate_ref, ret_ref, g_ref, gb_ref, gc_ref, x_ref, gcp_ref, xp_ref, gcn_ref, xn_ref,
                     cw_ref, w_ref, o_ref, *, seq_blocks):
    i = pl.program_id(0)
    tm = h_ref.shape[0]
    u = gc_ref[...] * x_ref[...]
    seq_pos = i % seq_blocks
    keep_prev = jnp.where(seq_pos == 0, 0.0, 1.0)
    keep_next = jnp.where(seq_pos == seq_blocks - 1, 0.0, 1.0)
    u_prev = gcp_ref[7:8, :] * xp_ref[7:8, :] * keep_prev
    u_next = gcn_ref[0:1, :] * xn_ref[0:1, :] * keep_next
    rid = lax.broadcasted_iota(jnp.int32, u.shape, 0)
    up = jnp.where(rid == 0, u_prev, pltpu.roll(u, 1, 0))
    un = jnp.where(rid == tm - 1, u_next, pltpu.roll(u, tm - 1, 0))
    conv = cw_ref[0:1, :] * up + cw_ref[1:2, :] * u + cw_ref[2:3, :] * un
    parts = []
    for hh in range(RET_HEADS):
        lanes = slice(hh * RET_V_DIM, (hh + 1) * RET_V_DIM)
        r = ret_ref[:, lanes]
        g = g_ref[:, lanes]
        parts.append(r * lax.rsqrt(jnp.mean(r * r, axis=-1, keepdims=True) + EPS) * (g / (1.0 + jnp.exp(-g))))
    y = jnp.concatenate(parts + [gb_ref[...] * conv], axis=1)
    o_ref[...] = h_ref[...] + gate_ref[0] * _mm(y, w_ref[...])


def _even_output(h, gate, ret, p, conv_w, w_out, seq_len, rows_per_mod):
    n, d = h.shape
    tm = ROW_TILE
    cw = CONV_CH
    tiles = tm // 8
    last_tile = n // 8 - 1
    rows = lambda width, j: pl.BlockSpec((tm, width), lambda i: (i, j))
    prev = lambda j: pl.BlockSpec((8, cw), lambda i: (jnp.maximum(i * tiles - 1, 0), j))
    nxt = lambda j: pl.BlockSpec((8, cw), lambda i: (jnp.minimum((i + 1) * tiles, last_tile), j))
    mod = pl.BlockSpec((1, 1, d), lambda i: ((i * tm) // rows_per_mod, 0, 0))
    return pl.pallas_call(
        functools.partial(_even_out_kernel, seq_blocks=seq_len // tm),
        grid=(n // tm,),
        in_specs=[rows(d, 0), mod, rows(RET_V_W, 0), rows(cw, 2), rows(cw, 3), rows(cw, 4), rows(cw, 5),
                  prev(4), prev(5), nxt(4), nxt(5), _full((CONV_K, cw)), _full((d, d))],
        out_specs=rows(d, 0),
        out_shape=jax.ShapeDtypeStruct((n, d), F32),
    )(h, gate, ret, p, p, p, p, p, p, p, p, conv_w, w_out.astype(BF))


def _attn_kernel(sink_ref, h_ref, gate_ref, q_ref, qs_ref, kp_ref, kc_ref, kn_ref, ksp_ref, ksc_ref, ksn_ref,
                 vp_ref, vc_ref, vn_ref, kx_ref, vx_ref, cq_ref, sq_ref, ckp_ref, ckc_ref, ckn_ref,
                 skp_ref, skc_ref, skn_ref, w_ref, o_ref, *, n_lat):
    j = pl.program_id(1)
    blk = ATT_BLOCK
    span = blk + 2 * WINDOW
    n_keys = span + kx_ref.shape[0]
    scale = ATT_HEAD_DIM ** -0.5
    q = (q_ref[...] * cq_ref[...] + qs_ref[...] * sq_ref[...]) * scale
    keys = jnp.concatenate([kp_ref[...] * ckp_ref[...] + ksp_ref[...] * skp_ref[...],
                            kc_ref[...] * ckc_ref[...] + ksc_ref[...] * skc_ref[...],
                            kn_ref[...] * ckn_ref[...] + ksn_ref[...] * skn_ref[...],
                            kx_ref[...]], axis=0).astype(BF)
    vals = jnp.concatenate([vp_ref[...], vc_ref[...], vn_ref[...], vx_ref[...]], axis=0).astype(BF)
    qpos = lax.broadcasted_iota(jnp.int32, (blk, n_keys), 0)
    r = lax.broadcasted_iota(jnp.int32, (blk, n_keys), 1)
    key_pos = (j - 1) * blk + r
    in_band = (r >= qpos) & (r <= qpos + 2 * WINDOW) & (key_pos >= 0) & (key_pos < n_lat)
    valid = jnp.concatenate([in_band | (r >= span)] * ATT_GROUP, axis=0)
    heads = [None] * ATT_HEADS
    for kh in range(ATT_KV_HEADS):
        kv = slice(kh * ATT_HEAD_DIM, (kh + 1) * ATT_HEAD_DIM)
        qg = jnp.concatenate([q[:, (kh * ATT_GROUP + g) * ATT_HEAD_DIM:(kh * ATT_GROUP + g + 1) * ATT_HEAD_DIM]
                              for g in range(ATT_GROUP)], axis=0)
        s = lax.dot_general(qg.astype(BF), keys[:, kv], NT, preferred_element_type=F32)
        s = jnp.where(valid, s, NEG_INF)
        sink = jnp.concatenate([jnp.full((blk, 1), sink_ref[kh * ATT_GROUP + g], F32) for g in range(ATT_GROUP)], axis=0)
        m = jnp.maximum(jnp.max(s, axis=-1, keepdims=True), sink)
        p = jnp.exp(s - m)
        den = jnp.sum(p, axis=-1, keepdims=True) + jnp.exp(sink - m)
        o = _mm(p, vals[:, kv]) / den
        for g in range(ATT_GROUP):
            heads[kh * ATT_GROUP + g] = o[g * blk:(g + 1) * blk]
    o_ref[...] = h_ref[...] + gate_ref[0] * _mm(jnp.concatenate(heads, axis=1), w_ref[...])


def _attention(h, gate, p_lat, p_ctx, sinks, cos, sin, w_out, batch):
    n, d = h.shape
    s = n // batch
    c = p_ctx.shape[0] // batch
    blk = ATT_BLOCK
    nb = s // blk
    kw = ATT_KV_W
    k0 = 2 * ATT_Q_W // kw
    row = lambda width, col: pl.BlockSpec((blk, width), lambda b, j: (b * nb + j, col))
    prv = lambda width, col: pl.BlockSpec((blk, width), lambda b, j: (b * nb + jnp.maximum(j - 1, 0), col))
    nxt = lambda width, col: pl.BlockSpec((blk, width), lambda b, j: (b * nb + jnp.minimum(j + 1, nb - 1), col))
    tab = lambda width: pl.BlockSpec((blk, width), lambda b, j: (j, 0))
    tab_p = lambda width: pl.BlockSpec((blk, width), lambda b, j: (jnp.maximum(j - 1, 0), 0))
    tab_n = lambda width: pl.BlockSpec((blk, width), lambda b, j: (jnp.minimum(j + 1, nb - 1), 0))
    ctx = lambda col: pl.BlockSpec((c, kw), lambda b, j: (b, col))
    mod = pl.BlockSpec((1, 1, d), lambda b, j: (b, 0, 0))
    return pl.pallas_call(
        functools.partial(_attn_kernel, n_lat=s),
        grid=(batch, nb),
        in_specs=[pl.BlockSpec(memory_space=pltpu.SMEM), row(d, 0), mod, row(ATT_Q_W, 0), row(ATT_Q_W, 1),
                  prv(kw, k0), row(kw, k0), nxt(kw, k0), prv(kw, k0 + 1), row(kw, k0 + 1), nxt(kw, k0 + 1),
                  prv(kw, k0 + 2), row(kw, k0 + 2), nxt(kw, k0 + 2), ctx(0), ctx(1),
                  tab(ATT_Q_W), tab(ATT_Q_W), tab_p(kw), tab(kw), tab_n(kw), tab_p(kw), tab(kw), tab_n(kw),
                  pl.BlockSpec((d, d), lambda b, j: (0, 0))],
        out_specs=row(d, 0),
        out_shape=jax.ShapeDtypeStruct((n, d), F32),
    )(sinks, h, gate, p_lat, p_lat, p_lat, p_lat, p_lat, p_lat, p_lat, p_lat, p_lat, p_lat, p_lat, p_ctx, p_ctx,
      cos, sin, cos, cos, cos, sin, sin, sin, w_out.astype(BF))


def _pack_bf16_table(tab):
    e = tab.shape[0]
    bits = lax.bitcast_convert_type(tab.astype(BF), jnp.uint16).astype(jnp.uint32)
    word = (bits[:, HALF_D:] << 16) | bits[:, :HALF_D]
    return lax.bitcast_convert_type(word, jnp.int32).reshape(e * SLAB, 128)


def _unpack_words(words):
    lo = lax.bitcast_convert_type(words << 16, F32)
    hi = lax.bitcast_convert_type(words & BF16_HI_MASK, F32)
    return lo, hi


def _load_slabs(tab_ref, slot_idx, k):
    return jnp.concatenate([tab_ref[pl.ds(pl.multiple_of(slot_idx[k + j], SLAB), SLAB), :] for j in range(PAIR)], axis=0)


def _extract_top(s, order, n):
    vals, ids = [], []
    for _ in range(n):
        m = jnp.max(s, axis=0, keepdims=True)
        first = jnp.min(jnp.where(s == m, order, ID_BIG), axis=0, keepdims=True)
        vals.append(m)
        ids.append(first)
        s = jnp.where(order == first, -jnp.inf, s)
    return vals, ids


def _route_kernel(h_ref, g_ref, sh_ref, sc_ref, wq_ref, k1_ref, k2_ref, f_ref, idx_ref, gate_ref,
                  v1_ref, v2_ref, i1_ref, i2_ref, et_ref, gt_ref):
    tb = h_ref.shape[0]
    f = _norm_mod(h_ref[...], g_ref[...], sh_ref[0], sc_ref[0])
    f_ref[...] = f
    qb = _mm(f, wq_ref[...]).astype(BF)
    key_id = lax.broadcasted_iota(jnp.int32, (PEER_N_KEYS, tb), 0).astype(F32)
    r8 = lax.broadcasted_iota(jnp.int32, (8, tb), 0).astype(F32)
    flat = jnp.concatenate([r8 * 16, (r8 + 8) * 16, r8 * 16 + 1, r8 + 8, r8, r8 + 16, r8 + 32, r8 + 48, r8 + 64], axis=0)
    for h in range(PEER_HEADS):
        qh = qb[:, h * PEER_D_KEY:(h + 1) * PEER_D_KEY]
        s1 = lax.dot_general(k1_ref[h], qh, NT, preferred_element_type=F32)
        s2 = lax.dot_general(k2_ref[h], qh, NT, preferred_element_type=F32)
        for s, v_ref, i_ref in ((s1, v1_ref, i1_ref), (s2, v2_ref, i2_ref)):
            vals, ids = _extract_top(s, key_id, PEER_TOPK)
            for k in range(PEER_TOPK):
                v_ref[k:k + 1, :] = vals[k]
                i_ref[k:k + 1, :] = ids[k]
        v1a, v1b, v2a, v2b = v1_ref[0:8, :], v1_ref[8:16, :], v2_ref[0:8, :], v2_ref[8:16, :]
        i1a, i1b, i2a, i2b = i1_ref[0:8, :], i1_ref[8:16, :], i2_ref[0:8, :], i2_ref[8:16, :]
        ninf = -jnp.inf
        cand = jnp.concatenate([
            v1a + v2a[0:1], v1b + v2a[0:1], v1a + v2a[1:2], v2b + v1a[0:1],
            jnp.where(r8 >= 2, v2a + v1a[0:1], ninf),
            jnp.where(r8 >= 2, v2a + v1a[1:2], ninf),
            jnp.where((r8 >= 2) & (r8 <= 4), v2a + v1a[2:3], ninf),
            jnp.where((r8 >= 2) & (r8 <= 3), v2a + v1a[3:4], ninf),
            jnp.where(r8 == 2, v2a + v1a[4:5], ninf)], axis=0)
        expert = jnp.concatenate([
            i1a * PEER_N_KEYS + i2a[0:1], i1b * PEER_N_KEYS + i2a[0:1], i1a * PEER_N_KEYS + i2a[1:2],
            i1a[0:1] * PEER_N_KEYS + i2b,
            i1a[0:1] * PEER_N_KEYS + i2a, i1a[1:2] * PEER_N_KEYS + i2a, i1a[2:3] * PEER_N_KEYS + i2a,
            i1a[3:4] * PEER_N_KEYS + i2a, i1a[4:5] * PEER_N_KEYS + i2a], axis=0)
        cs, picks = _extract_top(cand, flat, PEER_TOPK)
        ex = [jnp.exp(c - cs[0]) for c in cs]
        den = ex[0]
        for e in ex[1:]:
            den = den + e
        for k in range(PEER_TOPK):
            row = h * PEER_TOPK + k
            pick = jnp.max(jnp.where(flat == picks[k], expert, -1.0), axis=0, keepdims=True)
            et_ref[row:row + 1, :] = (pick * SLAB).astype(jnp.int32)
            gt_ref[row:row + 1, :] = ex[k] / den
    idx_ref[...] = et_ref[...].T
    gate_ref[...] = gt_ref[...].T


def _peer_route(h, gain, shift, scale, rows_per_mod, wq, k1, k2):
    t, d = h.shape
    tb = ROUTE_TOKENS
    mod = pl.BlockSpec((1, 1, d), lambda i: ((i * tb) // rows_per_mod, 0, 0))
    slots = pl.BlockSpec((tb, N_SLOTS), lambda i: (i, 0))
    return pl.pallas_call(
        _route_kernel,
        grid=(t // tb,),
        in_specs=[pl.BlockSpec((tb, d), lambda i: (i, 0)), _full((1, d)), mod, mod,
                  _full((d, PEER_HEADS * PEER_D_KEY)), _full(k1.shape), _full(k2.shape)],
        out_specs=[pl.BlockSpec((tb, d), lambda i: (i, 0)), slots, slots],
        out_shape=[jax.ShapeDtypeStruct((t, d), F32), jax.ShapeDtypeStruct((t, N_SLOTS), jnp.int32),
                   jax.ShapeDtypeStruct((t, N_SLOTS), F32)],
        scratch_shapes=[pltpu.VMEM((PEER_TOPK, tb), F32), pltpu.VMEM((PEER_TOPK, tb), F32),
                        pltpu.VMEM((PEER_TOPK, tb), F32), pltpu.VMEM((PEER_TOPK, tb), F32),
                        pltpu.VMEM((N_SLOTS, tb), jnp.int32), pltpu.VMEM((N_SLOTS, tb), F32)],
    )(h, gain.reshape(1, d), shift, scale, wq, k1, k2)


def _peer_u_kernel(idx_ref, x_ref, gate_ref, tab_ref, w_ref, p_ref, r_ref):
    tb = x_ref.shape[0]
    rows = N_SLOTS * SLAB

    ones = jnp.ones((8, 128), BF)
    for t in range(tb):
        x = x_ref[t]
        xlo = jnp.concatenate([x[0:SLAB]] * PAIR, axis=0)
        xhi = jnp.concatenate([x[SLAB:2 * SLAB]] * PAIR, axis=0)
        slot_idx = idx_ref.at[t]
        for k in range(0, N_SLOTS, PAIR):
            lo, hi = _unpack_words(_load_slabs(tab_ref, slot_idx, k))
            p_ref[pl.ds(t * rows + k * SLAB, PAIR * SLAB), :] = lo * xlo + hi * xhi
        part = p_ref[pl.ds(t * rows, N_SLOTS, stride=SLAB), :]
        for s in range(1, SLAB):
            part = part + p_ref[pl.ds(t * rows + s, N_SLOTS, stride=SLAB), :]
        hi = part.astype(BF)
        lo = (part - hi.astype(F32)).astype(BF)
        r_ref[t:t + 1, :] = (lax.dot_general(ones, hi, NT, preferred_element_type=F32)
                             + lax.dot_general(ones, lo, NT, preferred_element_type=F32))[0:1]
    r = r_ref[...]
    w_ref[...] = 0.5 * r * (1.0 + lax.erf(r * SQRT_HALF)) * gate_ref[...]


def _peer_v_kernel(idx_ref, w_ref, tab_ref, y_ref, wrep_ref):
    tb = y_ref.shape[0]
    n_acc = 4
    upper = lax.broadcasted_iota(jnp.int32, (PAIR * SLAB, 128), 0) >= SLAB
    for t in range(tb):
        wrep_ref[t] = jnp.broadcast_to(w_ref[t:t + 1, :], (N_SLOTS, 128)).T

    for t in range(tb):
        acc_lo = [jnp.zeros((PAIR * SLAB, 128), F32) for _ in range(n_acc)]
        acc_hi = [jnp.zeros((PAIR * SLAB, 128), F32) for _ in range(n_acc)]
        slot_idx = idx_ref.at[t]
        for k in range(0, N_SLOTS, PAIR):
            lo, hi = _unpack_words(_load_slabs(tab_ref, slot_idx, k))
            w = jnp.where(upper, wrep_ref[t, pl.ds(k + 1, 1), :], wrep_ref[t, pl.ds(k, 1), :])
            j = (k // PAIR) % n_acc
            acc_lo[j] = acc_lo[j] + w * lo
            acc_hi[j] = acc_hi[j] + w * hi
        lo = (acc_lo[0] + acc_lo[1]) + (acc_lo[2] + acc_lo[3])
        hi = (acc_hi[0] + acc_hi[1]) + (acc_hi[2] + acc_hi[3])
        y_ref[t, 0:SLAB, :] = lo[0:SLAB] + lo[SLAB:2 * SLAB]
        y_ref[t, SLAB:2 * SLAB, :] = hi[0:SLAB] + hi[SLAB:2 * SLAB]


def _peer_experts(f, idx, gate, u_words, v_words):
    t, d = f.shape
    tb = PEER_TOKENS
    smem = pl.BlockSpec((tb, N_SLOTS), lambda i: (i, 0), memory_space=pltpu.SMEM)
    slots = pl.BlockSpec((tb, N_SLOTS), lambda i: (i, 0))
    resident = pl.BlockSpec(memory_space=pltpu.VMEM)
    rows3 = pl.BlockSpec((tb, 2 * SLAB, 128), lambda i: (i, 0, 0))
    params = pltpu.CompilerParams(vmem_limit_bytes=VMEM_TABLE_LIMIT)
    w = pl.pallas_call(
        _peer_u_kernel,
        grid=(t // tb,),
        in_specs=[smem, rows3, slots, resident],
        out_specs=slots,
        out_shape=jax.ShapeDtypeStruct((t, N_SLOTS), F32),
        scratch_shapes=[pltpu.VMEM((tb * N_SLOTS * SLAB, 128), F32), pltpu.VMEM((tb, N_SLOTS), F32)],
        compiler_params=params,
    )(idx, f.reshape(t, 2 * SLAB, 128), gate, u_words)
    y = pl.pallas_call(
        _peer_v_kernel,
        grid=(t // tb,),
        in_specs=[smem, slots, resident],
        out_specs=rows3,
        out_shape=jax.ShapeDtypeStruct((t, 2 * SLAB, 128), F32),
        scratch_shapes=[pltpu.VMEM((tb, N_SLOTS, 128), F32)],
        compiler_params=params,
    )(idx, w, v_words)
    return y.reshape(t, d)


class _PeerWeights:
    def __init__(self, wq, keys1, keys2, u_tab, v_tab):
        half = PEER_D_KEY // 2
        self.wq = wq.astype(BF)
        self.k1 = jnp.pad(keys1, ((0, 0), (0, 0), (0, half))).astype(BF)
        self.k2 = jnp.pad(keys2, ((0, 0), (0, 0), (half, 0))).astype(BF)
        self.u = _pack_bf16_table(u_tab)
        self.v = _pack_bf16_table(v_tab)


def _peer_ffn(h, gain, shift, scale, rows_per_mod, pw):
    f, idx, gate = _peer_route(h, gain, shift, scale, rows_per_mod, pw.wq, pw.k1, pw.k2)
    return _peer_experts(f, idx, gate, pw.u, pw.v)


def _final_kernel(h_ref, y_ref, gate_ref, g_ref, o_ref):
    x = h_ref[...] + gate_ref[0] * y_ref[...]
    o_ref[...] = x * lax.rsqrt(jnp.mean(x * x, axis=-1, keepdims=True) + EPS) * g_ref[...]


def _final_norm(h, y, gate, gain, rows_per_mod):
    n, d = h.shape
    tm = ROW_TILE
    rows = pl.BlockSpec((tm, d), lambda i: (i, 0))
    mod = pl.BlockSpec((1, 1, d), lambda i: ((i * tm) // rows_per_mod, 0, 0))
    return pl.pallas_call(
        _final_kernel, grid=(n // tm,), in_specs=[rows, rows, mod, _full((1, d))], out_specs=rows,
        out_shape=jax.ShapeDtypeStruct((n, d), F32),
    )(h, y, gate, gain.reshape(1, d))


def kernel(x, c, ctx, c_ctx, ada_w, ada_b, mix_norm_g, ffn_norm_g, ev_w_in, ev_w_out,
           ret_decay_logit_f, ret_decay_logit_b, conv_w, od_w_in, od_w_out, attn_sinks,
           peer_wq, peer_keys1, peer_keys2, peer_u, peer_v, final_norm_g):
    batch, s, d = x.shape
    n_ctx = ctx.shape[1]
    assert DEPTH == 2 and s % ROW_TILE == 0 and n_ctx % ROW_TILE == 0 and batch + 1 <= ADA_ROWS
    h_lat = x.reshape(batch * s, d)
    h_ctx = ctx.reshape(batch * n_ctx, d)
    c_rows = jnp.zeros((ADA_ROWS, d), F32).at[:batch].set(c).at[batch].set(c_ctx)

    def modulation(layer):
        mod = _ada_modulation(c_rows, ada_w[layer], ada_b[layer])
        lat = [m.reshape(batch, 1, d) for m in jnp.split(mod[:batch], N_ADA, axis=-1)]
        cx = [m.reshape(1, 1, d) for m in jnp.split(mod[batch:batch + 1], N_ADA, axis=-1)]
        return lat, cx

    (sh1, sc1, g1, sh2, sc2, g2), (csh1, csc1, cg1, csh2, csc2, cg2) = modulation(0)
    w_in = ev_w_in[0]
    swap = _swap_columns(RET_HEADS, RET_QK_DIM)
    w0 = jnp.concatenate([w_in, w_in[:, :RET_Q_W][:, swap], w_in[:, RET_Q_W:2 * RET_Q_W][:, swap]], axis=1)
    p_lat = _in_projection(h_lat, mix_norm_g[0], sh1, sc1, w0, s)
    p_ctx = _in_projection(h_ctx, mix_norm_g[0], csh1, csc1, w0, batch * n_ctx)
    log_f = jax.nn.log_sigmoid(ret_decay_logit_f[0].astype(F32))
    log_b = jax.nn.log_sigmoid(ret_decay_logit_b[0].astype(F32))
    cos, sin = _rope_tables(s, RET_QK_DIM, RET_HEADS)
    ret_lat, ret_ctx = _retention(p_lat, p_ctx, log_f, log_b, cos, sin, batch)
    h_lat = _even_output(h_lat, g1, ret_lat, p_lat, conv_w[0], ev_w_out[0], s, s)
    h_ctx = _even_output(h_ctx, cg1, ret_ctx, p_ctx, conv_w[0], ev_w_out[0], n_ctx, batch * n_ctx)
    pw = _PeerWeights(peer_wq[0], peer_keys1[0], peer_keys2[0], peer_u[0], peer_v[0])
    y_lat = _peer_ffn(h_lat, ffn_norm_g[0], sh2, sc2, s, pw)
    y_ctx = _peer_ffn(h_ctx, ffn_norm_g[0], csh2, csc2, batch * n_ctx, pw)

    (sh1, sc1, g1, sh2, sc2, g2b), (csh1, csc1, _, _, _, _) = modulation(1)
    w_in = od_w_in[0]
    wq_cols, wk_cols, wv_cols = w_in[:, :ATT_Q_W], w_in[:, ATT_Q_W:ATT_Q_W + ATT_KV_W], w_in[:, ATT_Q_W + ATT_KV_W:]
    w1 = jnp.concatenate([wq_cols, wq_cols[:, _swap_columns(ATT_HEADS, ATT_HEAD_DIM)], wk_cols,
                          wk_cols[:, _swap_columns(ATT_KV_HEADS, ATT_HEAD_DIM)], wv_cols], axis=1)
    h_lat, p_lat = _in_projection(h_lat, mix_norm_g[1], sh1, sc1, w1, s, add=(y_lat, g2), emit_h=True)
    p_ctx = _in_projection(h_ctx, mix_norm_g[1], csh1, csc1, w_in[:, ATT_Q_W:], batch * n_ctx, add=(y_ctx, cg2))
    cos, sin = _rope_tables(s, ATT_HEAD_DIM, ATT_HEADS)
    h_lat = _attention(h_lat, g1, p_lat, p_ctx, attn_sinks[0].astype(F32), cos, sin, od_w_out[0], batch)
    pw = _PeerWeights(peer_wq[1], peer_keys1[1], peer_keys2[1], peer_u[1], peer_v[1])
    y_lat = _peer_ffn(h_lat, ffn_norm_g[1], sh2, sc2, s, pw)
    out = _final_norm(h_lat, y_lat, g2b, final_norm_g, s)
    return out.reshape(batch, s, d)
```

```python
import functools

import numpy as np
import jax
import jax.numpy as jnp
from jax import lax
from jax.experimental import pallas as pl
from jax.experimental.pallas import tpu as pltpu

D_MODEL = 1024
DEPTH = 2
GRID_W = 64
EPS = 1e-6
ROPE_BASE = 10000.0
NEG_INF = -1e30
N_ADA = 6
RET_HEADS = 4
RET_V_DIM = D_MODEL // (2 * RET_HEADS)
RET_QK_DIM = RET_V_DIM // 2
RET_CHUNK = 128
CONV_CH = D_MODEL // 2
CONV_K = 3
RET_Q_W = RET_HEADS * RET_QK_DIM
RET_V_W = RET_HEADS * RET_V_DIM
EV_COLS = 2 * RET_Q_W + 2 * RET_V_W + 3 * CONV_CH
ATT_HEADS = 16
ATT_HEAD_DIM = D_MODEL // ATT_HEADS
ATT_KV_HEADS = 4
ATT_GROUP = ATT_HEADS // ATT_KV_HEADS
WINDOW = 128
ATT_BLOCK = 128
ATT_Q_W = ATT_HEADS * ATT_HEAD_DIM
ATT_KV_W = ATT_KV_HEADS * ATT_HEAD_DIM
PEER_HEADS = 8
PEER_N_KEYS = 128
PEER_D_KEY = 128
PEER_TOPK = 16
ROUTE_TOKENS = 128
PEER_TOKENS = 32
N_SLOTS = PEER_HEADS * PEER_TOPK
HALF_D = D_MODEL // 2
SLAB = HALF_D // 128
PAIR = 8 // SLAB
BF16_HI_MASK = -65536
VMEM_TABLE_LIMIT = 48 * 1024 * 1024
VMEM_MIXER_LIMIT = 40 * 1024 * 1024
ID_BIG = 1e9
SQRT_HALF = 0.7071067811865476
ROW_TILE = 256
ADA_ROWS = 40

NT = (((1,), (1,)), ((), ()))
TN = (((0,), (0,)), ((), ()))
BF = jnp.bfloat16
F32 = jnp.float32


def _full(shape):
    return pl.BlockSpec(shape, lambda *_: (0,) * len(shape))


def _mm(a, b):
    return jnp.dot(a.astype(BF), b.astype(BF), preferred_element_type=F32)


def _rope_tables(n_tok, head_dim, n_heads):
    n_rows = n_tok // GRID_W
    rows = jnp.broadcast_to(jnp.arange(n_rows, dtype=F32)[:, None], (n_rows, GRID_W)).reshape(-1)
    cols = jnp.broadcast_to(jnp.arange(GRID_W, dtype=F32)[None, :], (n_rows, GRID_W)).reshape(-1)
    n_freq = head_dim // 4
    inv_freq = ROPE_BASE ** (-jnp.arange(n_freq, dtype=F32) / n_freq)
    ar, ac = rows[:, None] * inv_freq, cols[:, None] * inv_freq
    cos = jnp.concatenate([jnp.cos(ar), jnp.cos(ar), jnp.cos(ac), jnp.cos(ac)], axis=-1)
    sin = jnp.concatenate([-jnp.sin(ar), jnp.sin(ar), -jnp.sin(ac), jnp.sin(ac)], axis=-1)
    return jnp.tile(cos, (1, n_heads)), jnp.tile(sin, (1, n_heads))


def _swap_columns(n_heads, head_dim):
    q = head_dim // 4
    base = np.concatenate([np.arange(q, 2 * q), np.arange(0, q), np.arange(3 * q, 4 * q), np.arange(2 * q, 3 * q)])
    return (np.arange(n_heads)[:, None] * head_dim + base[None, :]).reshape(-1)


def _ada_kernel(c_ref, w_ref, b_ref, o_ref):
    c = c_ref[...]
    s = c / (1.0 + jnp.exp(-c))
    o_ref[...] = _mm(s, w_ref[...]) + b_ref[...]


def _ada_modulation(c_rows, w, b):
    d, f = w.shape
    tn = f // 4
    return pl.pallas_call(
        _ada_kernel,
        grid=(f // tn,),
        in_specs=[_full((ADA_ROWS, d)), pl.BlockSpec((d, tn), lambda j: (0, j)), pl.BlockSpec((1, tn), lambda j: (0, j))],
        out_specs=pl.BlockSpec((ADA_ROWS, tn), lambda j: (0, j)),
        out_shape=jax.ShapeDtypeStruct((ADA_ROWS, f), F32),
    )(c_rows, w.astype(BF), b.reshape(1, f))


def _norm_mod(x, gain, shift, scale):
    n = x * lax.rsqrt(jnp.mean(x * x, axis=-1, keepdims=True) + EPS) * gain
    return n * (1.0 + scale) + shift


def _proj_kernel(*refs, has_add, emit_h):
    refs = list(refs)
    h_ref = refs.pop(0)
    x = h_ref[...]
    if has_add:
        y_ref, gt_ref = refs.pop(0), refs.pop(0)
        x = x + gt_ref[0] * y_ref[...]
    g_ref, sh_ref, sc_ref, w_ref = refs[:4]
    outs = refs[4:]
    if emit_h:
        outs.pop(0)[...] = x
    a = _norm_mod(x, g_ref[...], sh_ref[0], sc_ref[0])
    outs[0][...] = _mm(a, w_ref[...])


def _in_projection(h, gain, shift, scale, w, rows_per_mod, add=None, emit_h=False):
    n, d = h.shape
    f = w.shape[1]
    tm = ROW_TILE
    rows = pl.BlockSpec((tm, d), lambda i: (i, 0))
    mod = pl.BlockSpec((1, 1, d), lambda i: ((i * tm) // rows_per_mod, 0, 0))
    args, specs = [h], [rows]
    if add is not None:
        args += [add[0], add[1]]
        specs += [rows, mod]
    args += [gain.reshape(1, d), shift, scale, w.astype(BF)]
    specs += [_full((1, d)), mod, mod, _full((d, f))]
    out_shape = [jax.ShapeDtypeStruct((n, f), F32)]
    out_specs = [pl.BlockSpec((tm, f), lambda i: (i, 0))]
    if emit_h:
        out_shape.insert(0, jax.ShapeDtypeStruct((n, d), F32))
        out_specs.insert(0, rows)
    res = pl.pallas_call(
        functools.partial(_proj_kernel, has_add=add is not None, emit_h=emit_h),
        grid=(n // tm,), in_specs=specs, out_specs=out_specs, out_shape=out_shape,
        compiler_params=pltpu.CompilerParams(vmem_limit_bytes=VMEM_MIXER_LIMIT),
    )(*args)
    return res if emit_h else res[0]


def _retention_kernel(logf_ref, logb_ref, ql_ref, kl_ref, vl_ref, qs_ref, ks_ref, qc_ref, kc_ref, vc_ref,
                      cos_ref, sin_ref, ol_ref, oc_ref):
    c = RET_CHUNK
    n_lat = ql_ref.shape[0] // c
    n_ctx = qc_ref.shape[0] // c
    k_scale = RET_QK_DIM ** -0.5
    row = lax.broadcasted_iota(jnp.int32, (c, c), 0).astype(F32)
    col = lax.broadcasted_iota(jnp.int32, (c, c), 1).astype(F32)
    rowk = lax.broadcasted_iota(jnp.int32, (c, RET_QK_DIM), 0).astype(F32)

    def step(qh, kh, vh, state, dmat, xi, zeta, cdm):
        qb, vb = qh.astype(BF), vh.astype(BF)
        scores = lax.dot_general(qb, kh.astype(BF), NT, preferred_element_type=F32) * dmat
        out = _mm(scores, vb) + _mm(qb, state) * xi
        kv = lax.dot_general((kh * zeta).astype(BF), vb, TN, preferred_element_type=F32)
        return out, cdm * state + kv

    for backward in (False, True):
        for h in range(RET_HEADS):
            lg = (logb_ref if backward else logf_ref)[h]
            if backward:
                dmat = jnp.where(col >= row, jnp.exp(lg * (col - row)), 0.0)
                xi = jnp.exp(lg * (c - row))
                zeta = jnp.exp(lg * rowk)
            else:
                dmat = jnp.where(row >= col, jnp.exp(lg * (row - col)), 0.0)
                xi = jnp.exp(lg * (row + 1.0))
                zeta = jnp.exp(lg * (c - 1.0 - rowk))
            cdm = jnp.exp(jnp.full((RET_QK_DIM, RET_V_DIM), lg * c, F32))
            qk = slice(h * RET_QK_DIM, (h + 1) * RET_QK_DIM)
            vv = slice(h * RET_V_DIM, (h + 1) * RET_V_DIM)
            state = jnp.zeros((RET_QK_DIM, RET_V_DIM), F32)
            for n in (range(n_ctx - 1, -1, -1) if backward else range(n_ctx)):
                r = slice(n * c, (n + 1) * c)
                out, state = step(qc_ref[r, qk], kc_ref[r, qk] * k_scale, vc_ref[r, vv], state, dmat, xi, zeta, cdm)
                oc_ref[r, vv] = oc_ref[r, vv] + out if backward else out

            def lat_chunk(i, state):
                n = (n_lat - 1 - i) if backward else i
                r = pl.ds(pl.multiple_of(n * c, c), c)
                cs, sn = cos_ref[r, qk], sin_ref[r, qk]
                qh = ql_ref[r, qk] * cs + qs_ref[r, qk] * sn
                kh = (kl_ref[r, qk] * cs + ks_ref[r, qk] * sn) * k_scale
                out, state = step(qh, kh, vl_ref[r, vv], state, dmat, xi, zeta, cdm)
                ol_ref[r, vv] = ol_ref[r, vv] + out if backward else out
                return state

            lax.fori_loop(0, n_lat, lat_chunk, state)


def _retention(p_lat, p_ctx, log_f, log_b, cos, sin, batch):
    s = p_lat.shape[0] // batch
    c = p_ctx.shape[0] // batch
    qw = RET_Q_W
    swap0 = EV_COLS // qw
    smem = pl.BlockSpec(memory_space=pltpu.SMEM)
    lat = lambda width, j: pl.BlockSpec((s, width), lambda b: (b, j))
    ctx = lambda width, j: pl.BlockSpec((c, width), lambda b: (b, j))
    return pl.pallas_call(
        _retention_kernel,
        grid=(batch,),
        in_specs=[smem, smem, lat(qw, 0), lat(qw, 1), lat(RET_V_W, 1), lat(qw, swap0), lat(qw, swap0 + 1),
                  ctx(qw, 0), ctx(qw, 1), ctx(RET_V_W, 1), _full((s, qw)), _full((s, qw))],
        out_specs=[lat(RET_V_W, 0), ctx(RET_V_W, 0)],
        out_shape=[jax.ShapeDtypeStruct((batch * s, RET_V_W), F32), jax.ShapeDtypeStruct((batch * c, RET_V_W), F32)],
        compiler_params=pltpu.CompilerParams(vmem_limit_bytes=VMEM_MIXER_LIMIT),
    )(log_f, log_b, p_lat, p_lat, p_lat, p_lat, p_lat, p_ctx, p_ctx, p_ctx, cos, sin)


def _even_out_kernel(h_ref, gate_ref, ret_ref, g_ref, gb_ref, gc_ref, x_ref, gcp_ref, xp_ref, gcn_ref, xn_ref,
                     cw_ref, w_ref, o_ref, *, seq_blocks):
    i = pl.program_id(0)
    tm = h_ref.shape[0]
    u = gc_ref[...] * x_ref[...]
    seq_pos = i % seq_blocks
    keep_prev = jnp.where(seq_pos == 0, 0.0, 1.0)
    keep_next = jnp.where(seq_pos == seq_blocks - 1, 0.0, 1.0)
    u_prev = gcp_ref[7:8, :] * xp_ref[7:8, :] * keep_prev
    u_next = gcn_ref[0:1, :] * xn_ref[0:1, :] * keep_next
    rid = lax.broadcasted_iota(jnp.int32, u.shape, 0)
    up = jnp.where(rid == 0, u_prev, pltpu.roll(u, 1, 0))
    un = jnp.where(rid == tm - 1, u_next, pltpu.roll(u, tm - 1, 0))
    conv = cw_ref[0:1, :] * up + cw_ref[1:2, :] * u + cw_ref[2:3, :] * un
    parts = []
    for hh in range(RET_HEADS):
        lanes = slice(hh * RET_V_DIM, (hh + 1) * RET_V_DIM)
        r = ret_ref[:, lanes]
        g = g_ref[:, lanes]
        parts.append(r * lax.rsqrt(jnp.mean(r * r, axis=-1, keepdims=True) + EPS) * (g / (1.0 + jnp.exp(-g))))
    y = jnp.concatenate(parts + [gb_ref[...] * conv], axis=1)
    o_ref[...] = h_ref[...] + gate_ref[0] * _mm(y, w_ref[...])


def _even_output(h, gate, ret, p, conv_w, w_out, seq_len, rows_per_mod):
    n, d = h.shape
    tm = ROW_TILE
    cw = CONV_CH
    tiles = tm // 8
    last_tile = n // 8 - 1
    rows = lambda width, j: pl.BlockSpec((tm, width), lambda i: (i, j))
    prev = lambda j: pl.BlockSpec((8, cw), lambda i: (jnp.maximum(i * tiles - 1, 0), j))
    nxt = lambda j: pl.BlockSpec((8, cw), lambda i: (jnp.minimum((i + 1) * tiles, last_tile), j))
    mod = pl.BlockSpec((1, 1, d), lambda i: ((i * tm) // rows_per_mod, 0, 0))
    return pl.pallas_call(
        functools.partial(_even_out_kernel, seq_blocks=seq_len // tm),
        grid=(n // tm,),
        in_specs=[rows(d, 0), mod, rows(RET_V_W, 0), rows(cw, 2), rows(cw, 3), rows(cw, 4), rows(cw, 5),
                  prev(4), prev(5), nxt(4), nxt(5), _full((CONV_K, cw)), _full((d, d))],
        out_specs=rows(d, 0),
        out_shape=jax.ShapeDtypeStruct((n, d), F32),
    )(h, gate, ret, p, p, p, p, p, p, p, p, conv_w, w_out.astype(BF))


def _attn_kernel(sink_ref, h_ref, gate_ref, q_ref, qs_ref, kp_ref, kc_ref, kn_ref, ksp_ref, ksc_ref, ksn_ref,
                 vp_ref, vc_ref, vn_ref, kx_ref, vx_ref, cq_ref, sq_ref, ckp_ref, ckc_ref, ckn_ref,
                 skp_ref, skc_ref, skn_ref, w_ref, o_ref, *, n_lat):
    j = pl.program_id(1)
    blk = ATT_BLOCK
    span = blk + 2 * WINDOW
    n_keys = span + kx_ref.shape[0]
    scale = ATT_HEAD_DIM ** -0.5
    q = (q_ref[...] * cq_ref[...] + qs_ref[...] * sq_ref[...]) * scale
    keys = jnp.concatenate([kp_ref[...] * ckp_ref[...] + ksp_ref[...] * skp_ref[...],
                            kc_ref[...] * ckc_ref[...] + ksc_ref[...] * skc_ref[...],
                            kn_ref[...] * ckn_ref[...] + ksn_ref[...] * skn_ref[...],
                            kx_ref[...]], axis=0).astype(BF)
    vals = jnp.concatenate([vp_ref[...], vc_ref[...], vn_ref[...], vx_ref[...]], axis=0).astype(BF)
    qpos = lax.broadcasted_iota(jnp.int32, (blk, n_keys), 0)
    r = lax.broadcasted_iota(jnp.int32, (blk, n_keys), 1)
    key_pos = (j - 1) * blk + r
    in_band = (r >= qpos) & (r <= qpos + 2 * WINDOW) & (key_pos >= 0) & (key_pos < n_lat)
    valid = jnp.concatenate([in_band | (r >= span)] * ATT_GROUP, axis=0)
    heads = [None] * ATT_HEADS
    for kh in range(ATT_KV_HEADS):
        kv = slice(kh * ATT_HEAD_DIM, (kh + 1) * ATT_HEAD_DIM)
        qg = jnp.concatenate([q[:, (kh * ATT_GROUP + g) * ATT_HEAD_DIM:(kh * ATT_GROUP + g + 1) * ATT_HEAD_DIM]
                              for g in range(ATT_GROUP)], axis=0)
        s = lax.dot_general(qg.astype(BF), keys[:, kv], NT, preferred_element_type=F32)
        s = jnp.where(valid, s, NEG_INF)
        sink = jnp.concatenate([jnp.full((blk, 1), sink_ref[kh * ATT_GROUP + g], F32) for g in range(ATT_GROUP)], axis=0)
        m = jnp.maximum(jnp.max(s, axis=-1, keepdims=True), sink)
        p = jnp.exp(s - m)
        den = jnp.sum(p, axis=-1, keepdims=True) + jnp.exp(sink - m)
        o = _mm(p, vals[:, kv]) / den
        for g in range(ATT_GROUP):
            heads[kh * ATT_GROUP + g] = o[g * blk:(g + 1) * blk]
    o_ref[...] = h_ref[...] + gate_ref[0] * _mm(jnp.concatenate(heads, axis=1), w_ref[...])


def _attention(h, gate, p_lat, p_ctx, sinks, cos, sin, w_out, batch):
    n, d = h.shape
    s = n // batch
    c = p_ctx.shape[0] // batch
    blk = ATT_BLOCK
    nb = s // blk
    kw = ATT_KV_W
    k0 = 2 * ATT_Q_W // kw
    row = lambda width, col: pl.BlockSpec((blk, width), lambda b, j: (b * nb + j, col))
    prv = lambda width, col: pl.BlockSpec((blk, width), lambda b, j: (b * nb + jnp.maximum(j - 1, 0), col))
    nxt = lambda width, col: pl.BlockSpec((blk, width), lambda b, j: (b * nb + jnp.minimum(j + 1, nb - 1), col))
    tab = lambda width: pl.BlockSpec((blk, width), lambda b, j: (j, 0))
    tab_p = lambda width: pl.BlockSpec((blk, width), lambda b, j: (jnp.maximum(j - 1, 0), 0))
    tab_n = lambda width: pl.BlockSpec((blk, width), lambda b, j: (jnp.minimum(j + 1, nb - 1), 0))
    ctx = lambda col: pl.BlockSpec((c, kw), lambda b, j: (b, col))
    mod = pl.BlockSpec((1, 1, d), lambda b, j: (b, 0, 0))
    return pl.pallas_call(
        functools.partial(_attn_kernel, n_lat=s),
        grid=(batch, nb),
        in_specs=[pl.BlockSpec(memory_space=pltpu.SMEM), row(d, 0), mod, row(ATT_Q_W, 0), row(ATT_Q_W, 1),
                  prv(kw, k0), row(kw, k0), nxt(kw, k0), prv(kw, k0 + 1), row(kw, k0 + 1), nxt(kw, k0 + 1),
                  prv(kw, k0 + 2), row(kw, k0 + 2), nxt(kw, k0 + 2), ctx(0), ctx(1),
                  tab(ATT_Q_W), tab(ATT_Q_W), tab_p(kw), tab(kw), tab_n(kw), tab_p(kw), tab(kw), tab_n(kw),
                  pl.BlockSpec((d, d), lambda b, j: (0, 0))],
        out_specs=row(d, 0),
        out_shape=jax.ShapeDtypeStruct((n, d), F32),
    )(sinks, h, gate, p_lat, p_lat, p_lat, p_lat, p_lat, p_lat, p_lat, p_lat, p_lat, p_lat, p_lat, p_ctx, p_ctx,
      cos, sin, cos, cos, cos, sin, sin, sin, w_out.astype(BF))


def _pack_bf16_table(tab):
    e = tab.shape[0]
    bits = lax.bitcast_convert_type(tab.astype(BF), jnp.uint16).astype(jnp.uint32)
    word = (bits[:, HALF_D:] << 16) | bits[:, :HALF_D]
    rows = lax.bitcast_convert_type(word, jnp.int32).reshape(e * SLAB, 128)
    return jnp.pad(rows, ((SLAB, SLAB), (0, 0)))


def _unpack_words(words):
    lo = lax.bitcast_convert_type(words << 16, F32)
    hi = lax.bitcast_convert_type(words & BF16_HI_MASK, F32)
    return lo, hi


def _extract_top(s, order, n):
    vals, ids = [], []
    for _ in range(n):
        m = jnp.max(s, axis=0, keepdims=True)
        first = jnp.min(jnp.where(s == m, order, ID_BIG), axis=0, keepdims=True)
        vals.append(m)
        ids.append(first)
        s = jnp.where(order == first, -jnp.inf, s)
    return vals, ids


def _route_kernel(h_ref, g_ref, sh_ref, sc_ref, wq_ref, k1_ref, k2_ref, f_ref, idx_ref, gate_ref,
                  v1_ref, v2_ref, i1_ref, i2_ref, et_ref, gt_ref):
    tb = h_ref.shape[0]
    f = _norm_mod(h_ref[...], g_ref[...], sh_ref[0], sc_ref[0])
    f_ref[...] = f
    qb = _mm(f, wq_ref[...]).astype(BF)
    key_id = lax.broadcasted_iota(jnp.int32, (PEER_N_KEYS, tb), 0).astype(F32)
    r8 = lax.broadcasted_iota(jnp.int32, (8, tb), 0).astype(F32)
    flat = jnp.concatenate([r8 * 16, (r8 + 8) * 16, r8 * 16 + 1, r8 + 8, r8, r8 + 16, r8 + 32, r8 + 48, r8 + 64], axis=0)
    for h in range(PEER_HEADS):
        qh = qb[:, h * PEER_D_KEY:(h + 1) * PEER_D_KEY]
        s1 = lax.dot_general(k1_ref[h], qh, NT, preferred_element_type=F32)
        s2 = lax.dot_general(k2_ref[h], qh, NT, preferred_element_type=F32)
        for s, v_ref, i_ref in ((s1, v1_ref, i1_ref), (s2, v2_ref, i2_ref)):
            vals, ids = _extract_top(s, key_id, PEER_TOPK)
            for k in range(PEER_TOPK):
                v_ref[k:k + 1, :] = vals[k]
                i_ref[k:k + 1, :] = ids[k]
        v1a, v1b, v2a, v2b = v1_ref[0:8, :], v1_ref[8:16, :], v2_ref[0:8, :], v2_ref[8:16, :]
        i1a, i1b, i2a, i2b = i1_ref[0:8, :], i1_ref[8:16, :], i2_ref[0:8, :], i2_ref[8:16, :]
        ninf = -jnp.inf
        cand = jnp.concatenate([
            v1a + v2a[0:1], v1b + v2a[0:1], v1a + v2a[1:2], v2b + v1a[0:1],
            jnp.where(r8 >= 2, v2a + v1a[0:1], ninf),
            jnp.where(r8 >= 2, v2a + v1a[1:2], ninf),
            jnp.where((r8 >= 2) & (r8 <= 4), v2a + v1a[2:3], ninf),
            jnp.where((r8 >= 2) & (r8 <= 3), v2a + v1a[3:4], ninf),
            jnp.where(r8 == 2, v2a + v1a[4:5], ninf)], axis=0)
        expert = jnp.concatenate([
            i1a * PEER_N_KEYS + i2a[0:1], i1b * PEER_N_KEYS + i2a[0:1], i1a * PEER_N_KEYS + i2a[1:2],
            i1a[0:1] * PEER_N_KEYS + i2b,
            i1a[0:1] * PEER_N_KEYS + i2a, i1a[1:2] * PEER_N_KEYS + i2a, i1a[2:3] * PEER_N_KEYS + i2a,
            i1a[3:4] * PEER_N_KEYS + i2a, i1a[4:5] * PEER_N_KEYS + i2a], axis=0)
        cs, picks = _extract_top(cand, flat, PEER_TOPK)
        ex = [jnp.exp(c - cs[0]) for c in cs]
        den = ex[0]
        for e in ex[1:]:
            den = den + e
        for k in range(PEER_TOPK):
            row = h * PEER_TOPK + k
            pick = jnp.max(jnp.where(flat == picks[k], expert, -1.0), axis=0, keepdims=True)
            et_ref[row:row + 1, :] = (pick * SLAB).astype(jnp.int32)
            gt_ref[row:row + 1, :] = ex[k] / den
    idx_ref[...] = et_ref[...].T
    gate_ref[...] = gt_ref[...].T


def _peer_route(h, gain, shift, scale, rows_per_mod, wq, k1, k2):
    t, d = h.shape
    tb = ROUTE_TOKENS
    mod = pl.BlockSpec((1, 1, d), lambda i: ((i * tb) // rows_per_mod, 0, 0))
    slots = pl.BlockSpec((tb, N_SLOTS), lambda i: (i, 0))
    return pl.pallas_call(
        _route_kernel,
        grid=(t // tb,),
        in_specs=[pl.BlockSpec((tb, d), lambda i: (i, 0)), _full((1, d)), mod, mod,
                  _full((d, PEER_HEADS * PEER_D_KEY)), _full(k1.shape), _full(k2.shape)],
        out_specs=[pl.BlockSpec((tb, d), lambda i: (i, 0)), slots, slots],
        out_shape=[jax.ShapeDtypeStruct((t, d), F32), jax.ShapeDtypeStruct((t, N_SLOTS), jnp.int32),
                   jax.ShapeDtypeStruct((t, N_SLOTS), F32)],
        scratch_shapes=[pltpu.VMEM((PEER_TOPK, tb), F32), pltpu.VMEM((PEER_TOPK, tb), F32),
                        pltpu.VMEM((PEER_TOPK, tb), F32), pltpu.VMEM((PEER_TOPK, tb), F32),
                        pltpu.VMEM((N_SLOTS, tb), jnp.int32), pltpu.VMEM((N_SLOTS, tb), F32)],
    )(h, gain.reshape(1, d), shift, scale, wq, k1, k2)


def _load_slabs(tab_ref, slot_idx, k, upper):
    first = tab_ref[pl.ds(pl.multiple_of(slot_idx[k] + SLAB, SLAB), PAIR * SLAB), :]
    second = tab_ref[pl.ds(pl.multiple_of(slot_idx[k + 1], SLAB), PAIR * SLAB), :]
    return jnp.where(upper, second, first)


def _peer_u_kernel(idx_ref, x_ref, gate_ref, tab_ref, w_ref, p_ref, r_ref):
    tb = x_ref.shape[0]
    rows = N_SLOTS * SLAB
    upper = lax.broadcasted_iota(jnp.int32, (PAIR * SLAB, 128), 0) >= SLAB

    ones = jnp.ones((8, 128), BF)
    for t in range(tb):
        x = x_ref[t]
        xlo = jnp.concatenate([x[0:SLAB]] * PAIR, axis=0)
        xhi = jnp.concatenate([x[SLAB:2 * SLAB]] * PAIR, axis=0)
        slot_idx = idx_ref.at[t]
        for k in range(0, N_SLOTS, PAIR):
            lo, hi = _unpack_words(_load_slabs(tab_ref, slot_idx, k, upper))
            p_ref[pl.ds(t * rows + k * SLAB, PAIR * SLAB), :] = lo * xlo + hi * xhi
        part = p_ref[pl.ds(t * rows, N_SLOTS, stride=SLAB), :]
        for s in range(1, SLAB):
            part = part + p_ref[pl.ds(t * rows + s, N_SLOTS, stride=SLAB), :]
        hi = part.astype(BF)
        lo = (part - hi.astype(F32)).astype(BF)
        r_ref[t:t + 1, :] = (lax.dot_general(ones, hi, NT, preferred_element_type=F32)
                             + lax.dot_general(ones, lo, NT, preferred_element_type=F32))[0:1]
    r = r_ref[...]
    w_ref[...] = 0.5 * r * (1.0 + lax.erf(r * SQRT_HALF)) * gate_ref[...]


def _peer_v_kernel(idx_ref, w_ref, tab_ref, y_ref, wrep_ref):
    tb = y_ref.shape[0]
    n_acc = 4
    upper = lax.broadcasted_iota(jnp.int32, (PAIR * SLAB, 128), 0) >= SLAB
    for t in range(tb):
        wrep_ref[t] = jnp.broadcast_to(w_ref[t:t + 1, :], (N_SLOTS, 128)).T

    for t in range(tb):
        acc_lo = [jnp.zeros((PAIR * SLAB, 128), F32) for _ in range(n_acc)]
        acc_hi = [jnp.zeros((PAIR * SLAB, 128), F32) for _ in range(n_acc)]
        slot_idx = idx_ref.at[t]
        for k in range(0, N_SLOTS, PAIR):
            lo, hi = _unpack_words(_load_slabs(tab_ref, slot_idx, k, upper))
            w = jnp.where(upper, wrep_ref[t, pl.ds(k + 1, 1), :], wrep_ref[t, pl.ds(k, 1), :])
            j = (k // PAIR) % n_acc
            acc_lo[j] = acc_lo[j] + w * lo
            acc_hi[j] = acc_hi[j] + w * hi
        lo = (acc_lo[0] + acc_lo[1]) + (acc_lo[2] + acc_lo[3])
        hi = (acc_hi[0] + acc_hi[1]) + (acc_hi[2] + acc_hi[3])
        y_ref[t, 0:SLAB, :] = lo[0:SLAB] + lo[SLAB:2 * SLAB]
        y_ref[t, SLAB:2 * SLAB, :] = hi[0:SLAB] + hi[SLAB:2 * SLAB]


def _peer_experts(f, idx, gate, u_words, v_words):
    t, d = f.shape
    tb = PEER_TOKENS
    smem = pl.BlockSpec((tb, N_SLOTS), lambda i: (i, 0), memory_space=pltpu.SMEM, pipeline_mode=pl.Buffered(1))
    slots = pl.BlockSpec((tb, N_SLOTS), lambda i: (i, 0))
    resident = pl.BlockSpec(memory_space=pltpu.VMEM)
    rows3 = pl.BlockSpec((tb, 2 * SLAB, 128), lambda i: (i, 0, 0))
    params = pltpu.CompilerParams(vmem_limit_bytes=VMEM_TABLE_LIMIT)
    w = pl.pallas_call(
        _peer_u_kernel,
        grid=(t // tb,),
        in_specs=[smem, rows3, slots, resident],
        out_specs=slots,
        out_shape=jax.ShapeDtypeStruct((t, N_SLOTS), F32),
        scratch_shapes=[pltpu.VMEM((tb * N_SLOTS * SLAB, 128), F32), pltpu.VMEM((tb, N_SLOTS), F32)],
        compiler_params=params,
    )(idx, f.reshape(t, 2 * SLAB, 128), gate, u_words)
    y = pl.pallas_call(
        _peer_v_kernel,
        grid=(t // tb,),
        in_specs=[smem, slots, resident],
        out_specs=rows3,
        out_shape=jax.ShapeDtypeStruct((t, 2 * SLAB, 128), F32),
        scratch_shapes=[pltpu.VMEM((tb, N_SLOTS, 128), F32)],
        compiler_params=params,
    )(idx, w, v_words)
    return y.reshape(t, d)


class _PeerWeights:
    def __init__(self, wq, keys1, keys2, u_tab, v_tab):
        half = PEER_D_KEY // 2
        self.wq = wq.astype(BF)
        self.k1 = jnp.pad(keys1, ((0, 0), (0, 0), (0, half))).astype(BF)
        self.k2 = jnp.pad(keys2, ((0, 0), (0, 0), (half, 0))).astype(BF)
        self.u = _pack_bf16_table(u_tab)
        self.v = _pack_bf16_table(v_tab)


def _peer_ffn(h, gain, shift, scale, rows_per_mod, pw):
    f, idx, gate = _peer_route(h, gain, shift, scale, rows_per_mod, pw.wq, pw.k1, pw.k2)
    return _peer_experts(f, idx, gate, pw.u, pw.v)


def _final_kernel(h_ref, y_ref, gate_ref, g_ref, o_ref):
    x = h_ref[...] + gate_ref[0] * y_ref[...]
    o_ref[...] = x * lax.rsqrt(jnp.mean(x * x, axis=-1, keepdims=True) + EPS) * g_ref[...]


def _final_norm(h, y, gate, gain, rows_per_mod):
    n, d = h.shape
    tm = ROW_TILE
    rows = pl.BlockSpec((tm, d), lambda i: (i, 0))
    mod = pl.BlockSpec((1, 1, d), lambda i: ((i * tm) // rows_per_mod, 0, 0))
    return pl.pallas_call(
        _final_kernel, grid=(n // tm,), in_specs=[rows, rows, mod, _full((1, d))], out_specs=rows,
        out_shape=jax.ShapeDtypeStruct((n, d), F32),
    )(h, y, gate, gain.reshape(1, d))


def kernel(x, c, ctx, c_ctx, ada_w, ada_b, mix_norm_g, ffn_norm_g, ev_w_in, ev_w_out,
           ret_decay_logit_f, ret_decay_logit_b, conv_w, od_w_in, od_w_out, attn_sinks,
           peer_wq, peer_keys1, peer_keys2, peer_u, peer_v, final_norm_g):
    batch, s, d = x.shape
    n_ctx = ctx.shape[1]
    assert DEPTH == 2 and s % ROW_TILE == 0 and n_ctx % ROW_TILE == 0 and batch + 1 <= ADA_ROWS
    h_lat = x.reshape(batch * s, d)
    h_ctx = ctx.reshape(batch * n_ctx, d)
    c_rows = jnp.zeros((ADA_ROWS, d), F32).at[:batch].set(c).at[batch].set(c_ctx)

    def modulation(layer):
        mod = _ada_modulation(c_rows, ada_w[layer], ada_b[layer])
        lat = [m.reshape(batch, 1, d) for m in jnp.split(mod[:batch], N_ADA, axis=-1)]
        cx = [m.reshape(1, 1, d) for m in jnp.split(mod[batch:batch + 1], N_ADA, axis=-1)]
        return lat, cx

    (sh1, sc1, g1, sh2, sc2, g2), (csh1, csc1, cg1, csh2, csc2, cg2) = modulation(0)
    w_in = ev_w_in[0]
    swap = _swap_columns(RET_HEADS, RET_QK_DIM)
    w0 = jnp.concatenate([w_in, w_in[:, :RET_Q_W][:, swap], w_in[:, RET_Q_W:2 * RET_Q_W][:, swap]], axis=1)
    p_lat = _in_projection(h_lat, mix_norm_g[0], sh1, sc1, w0, s)
    p_ctx = _in_projection(h_ctx, mix_norm_g[0], csh1, csc1, w0, batch * n_ctx)
    log_f = jax.nn.log_sigmoid(ret_decay_logit_f[0].astype(F32))
    log_b = jax.nn.log_sigmoid(ret_decay_logit_b[0].astype(F32))
    cos, sin = _rope_tables(s, RET_QK_DIM, RET_HEADS)
    ret_lat, ret_ctx = _retention(p_lat, p_ctx, log_f, log_b, cos, sin, batch)
    h_lat = _even_output(h_lat, g1, ret_lat, p_lat, conv_w[0], ev_w_out[0], s, s)
    h_ctx = _even_output(h_ctx, cg1, ret_ctx, p_ctx, conv_w[0], ev_w_out[0], n_ctx, batch * n_ctx)
    pw = _PeerWeights(peer_wq[0], peer_keys1[0], peer_keys2[0], peer_u[0], peer_v[0])
    y_lat = _peer_ffn(h_lat, ffn_norm_g[0], sh2, sc2, s, pw)
    y_ctx = _peer_ffn(h_ctx, ffn_norm_g[0], csh2, csc2, batch * n_ctx, pw)

    (sh1, sc1, g1, sh2, sc2, g2b), (csh1, csc1, _, _, _, _) = modulation(1)
    w_in = od_w_in[0]
    wq_cols, wk_cols, wv_cols = w_in[:, :ATT_Q_W], w_in[:, ATT_Q_W:ATT_Q_W + ATT_KV_W], w_in[:, ATT_Q_W + ATT_KV_W:]
    w1 = jnp.concatenate([wq_cols, wq_cols[:, _swap_columns(ATT_HEADS, ATT_HEAD_DIM)], wk_cols,
                          wk_cols[:, _swap_columns(ATT_KV_HEADS, ATT_HEAD_DIM)], wv_cols], axis=1)
    h_lat, p_lat = _in_projection(h_lat, mix_norm_g[1], sh1, sc1, w1, s, add=(y_lat, g2), emit_h=True)
    p_ctx = _in_projection(h_ctx, mix_norm_g[1], csh1, csc1, w_in[:, ATT_Q_W:], batch * n_ctx, add=(y_ctx, cg2))
    cos, sin = _rope_tables(s, ATT_HEAD_DIM, ATT_HEADS)
    h_lat = _attention(h_lat, g1, p_lat, p_ctx, attn_sinks[0].astype(F32), cos, sin, od_w_out[0], batch)
    pw = _PeerWeights(peer_wq[1], peer_keys1[1], peer_keys2[1], peer_u[1], peer_v[1])
    y_lat = _peer_ffn(h_lat, ffn_norm_g[1], sh2, sc2, s, pw)
    out = _final_norm(h_lat, y_lat, g2b, final_norm_g, s)
    return out.reshape(batch, s, d)
```

```python
import functools

import numpy as np
import jax
import jax.numpy as jnp
from jax import lax
from jax.experimental import pallas as pl
from jax.experimental.pallas import tpu as pltpu

D_MODEL = 1024
DEPTH = 2
GRID_W = 64
EPS = 1e-6
ROPE_BASE = 10000.0
NEG_INF = -1e30
N_ADA = 6
RET_HEADS = 4
RET_V_DIM = D_MODEL // (2 * RET_HEADS)
RET_QK_DIM = RET_V_DIM // 2
RET_CHUNK = 128
CONV_CH = D_MODEL // 2
CONV_K = 3
RET_Q_W = RET_HEADS * RET_QK_DIM
RET_V_W = RET_HEADS * RET_V_DIM
EV_COLS = 2 * RET_Q_W + 2 * RET_V_W + 3 * CONV_CH
ATT_HEADS = 16
ATT_HEAD_DIM = D_MODEL // ATT_HEADS
ATT_KV_HEADS = 4
ATT_GROUP = ATT_HEADS // ATT_KV_HEADS
WINDOW = 128
ATT_BLOCK = 128
ATT_Q_W = ATT_HEADS * ATT_HEAD_DIM
ATT_KV_W = ATT_KV_HEADS * ATT_HEAD_DIM
PEER_HEADS = 8
PEER_N_KEYS = 128
PEER_D_KEY = 128
PEER_TOPK = 16
ROUTE_TOKENS = 128
PEER_TOKENS = 32
N_SLOTS = PEER_HEADS * PEER_TOPK
HALF_D = D_MODEL // 2
SLAB = HALF_D // 128
PAIR = 8 // SLAB
BF16_HI_MASK = -65536
VMEM_TABLE_LIMIT = 48 * 1024 * 1024
VMEM_MIXER_LIMIT = 40 * 1024 * 1024
ID_BIG = 1e9
SQRT_HALF = 0.7071067811865476
ROW_TILE = 256
ADA_ROWS = 40

NT = (((1,), (1,)), ((), ()))
TN = (((0,), (0,)), ((), ()))
BF = jnp.bfloat16
F32 = jnp.float32


def _full(shape):
    return pl.BlockSpec(shape, lambda *_: (0,) * len(shape))


def _mm(a, b):
    return jnp.dot(a.astype(BF), b.astype(BF), preferred_element_type=F32)


def _rope_tables(n_tok, head_dim, n_heads):
    n_rows = n_tok // GRID_W
    rows = jnp.broadcast_to(jnp.arange(n_rows, dtype=F32)[:, None], (n_rows, GRID_W)).reshape(-1)
    cols = jnp.broadcast_to(jnp.arange(GRID_W, dtype=F32)[None, :], (n_rows, GRID_W)).reshape(-1)
    n_freq = head_dim // 4
    inv_freq = ROPE_BASE ** (-jnp.arange(n_freq, dtype=F32) / n_freq)
    ar, ac = rows[:, None] * inv_freq, cols[:, None] * inv_freq
    cos = jnp.concatenate([jnp.cos(ar), jnp.cos(ar), jnp.cos(ac), jnp.cos(ac)], axis=-1)
    sin = jnp.concatenate([-jnp.sin(ar), jnp.sin(ar), -jnp.sin(ac), jnp.sin(ac)], axis=-1)
    return jnp.tile(cos, (1, n_heads)), jnp.tile(sin, (1, n_heads))


def _swap_columns(n_heads, head_dim):
    q = head_dim // 4
    base = np.concatenate([np.arange(q, 2 * q), np.arange(0, q), np.arange(3 * q, 4 * q), np.arange(2 * q, 3 * q)])
    return (np.arange(n_heads)[:, None] * head_dim + base[None, :]).reshape(-1)


def _ada_kernel(c_ref, w_ref, b_ref, o_ref):
    c = c_ref[...]
    s = c / (1.0 + jnp.exp(-c))
    o_ref[...] = _mm(s, w_ref[...]) + b_ref[...]


def _ada_modulation(c_rows, w, b):
    d, f = w.shape
    tn = f // 4
    return pl.pallas_call(
        _ada_kernel,
        grid=(f // tn,),
        in_specs=[_full((ADA_ROWS, d)), pl.BlockSpec((d, tn), lambda j: (0, j)), pl.BlockSpec((1, tn), lambda j: (0, j))],
        out_specs=pl.BlockSpec((ADA_ROWS, tn), lambda j: (0, j)),
        out_shape=jax.ShapeDtypeStruct((ADA_ROWS, f), F32),
    )(c_rows, w.astype(BF), b.reshape(1, f))


def _norm_mod(x, gain, shift, scale):
    n = x * lax.rsqrt(jnp.mean(x * x, axis=-1, keepdims=True) + EPS) * gain
    return n * (1.0 + scale) + shift


def _proj_kernel(*refs, has_add, emit_h):
    refs = list(refs)
    h_ref = refs.pop(0)
    x = h_ref[...]
    if has_add:
        y_ref, gt_ref = refs.pop(0), refs.pop(0)
        x = x + gt_ref[0] * y_ref[...]
    g_ref, sh_ref, sc_ref, w_ref = refs[:4]
    outs = refs[4:]
    if emit_h:
        outs.pop(0)[...] = x
    a = _norm_mod(x, g_ref[...], sh_ref[0], sc_ref[0])
    outs[0][...] = _mm(a, w_ref[...])


def _in_projection(h, gain, shift, scale, w, rows_per_mod, add=None, emit_h=False):
    n, d = h.shape
    f = w.shape[1]
    tm = ROW_TILE
    rows = pl.BlockSpec((tm, d), lambda i: (i, 0))
    mod = pl.BlockSpec((1, 1, d), lambda i: ((i * tm) // rows_per_mod, 0, 0))
    args, specs = [h], [rows]
    if add is not None:
        args += [add[0], add[1]]
        specs += [rows, mod]
    args += [gain.reshape(1, d), shift, scale, w.astype(BF)]
    specs += [_full((1, d)), mod, mod, _full((d, f))]
    out_shape = [jax.ShapeDtypeStruct((n, f), F32)]
    out_specs = [pl.BlockSpec((tm, f), lambda i: (i, 0))]
    if emit_h:
        out_shape.insert(0, jax.ShapeDtypeStruct((n, d), F32))
        out_specs.insert(0, rows)
    res = pl.pallas_call(
        functools.partial(_proj_kernel, has_add=add is not None, emit_h=emit_h),
        grid=(n // tm,), in_specs=specs, out_specs=out_specs, out_shape=out_shape,
        compiler_params=pltpu.CompilerParams(vmem_limit_bytes=VMEM_MIXER_LIMIT),
    )(*args)
    return res if emit_h else res[0]


def _retention_kernel(logf_ref, logb_ref, ql_ref, kl_ref, vl_ref, qs_ref, ks_ref, qc_ref, kc_ref, vc_ref,
                      cos_ref, sin_ref, ol_ref, oc_ref):
    c = RET_CHUNK
    n_lat = ql_ref.shape[0] // c
    n_ctx = qc_ref.shape[0] // c
    k_scale = RET_QK_DIM ** -0.5
    row = lax.broadcasted_iota(jnp.int32, (c, c), 0).astype(F32)
    col = lax.broadcasted_iota(jnp.int32, (c, c), 1).astype(F32)
    rowk = lax.broadcasted_iota(jnp.int32, (c, RET_QK_DIM), 0).astype(F32)

    def step(qh, kh, vh, state, dmat, xi, zeta, cdm):
        qb, vb = qh.astype(BF), vh.astype(BF)
        scores = lax.dot_general(qb, kh.astype(BF), NT, preferred_element_type=F32) * dmat
        out = _mm(scores, vb) + _mm(qb, state) * xi
        kv = lax.dot_general((kh * zeta).astype(BF), vb, TN, preferred_element_type=F32)
        return out, cdm * state + kv

    for backward in (False, True):
        for h in range(RET_HEADS):
            lg = (logb_ref if backward else logf_ref)[h]
            if backward:
                dmat = jnp.where(col >= row, jnp.exp(lg * (col - row)), 0.0)
                xi = jnp.exp(lg * (c - row))
                zeta = jnp.exp(lg * rowk)
            else:
                dmat = jnp.where(row >= col, jnp.exp(lg * (row - col)), 0.0)
                xi = jnp.exp(lg * (row + 1.0))
                zeta = jnp.exp(lg * (c - 1.0 - rowk))
            cdm = jnp.exp(jnp.full((RET_QK_DIM, RET_V_DIM), lg * c, F32))
            qk = slice(h * RET_QK_DIM, (h + 1) * RET_QK_DIM)
            vv = slice(h * RET_V_DIM, (h + 1) * RET_V_DIM)
            state = jnp.zeros((RET_QK_DIM, RET_V_DIM), F32)
            for n in (range(n_ctx - 1, -1, -1) if backward else range(n_ctx)):
                r = slice(n * c, (n + 1) * c)
                out, state = step(qc_ref[r, qk], kc_ref[r, qk] * k_scale, vc_ref[r, vv], state, dmat, xi, zeta, cdm)
                oc_ref[r, vv] = oc_ref[r, vv] + out if backward else out

            def lat_chunk(i, state):
                n = (n_lat - 1 - i) if backward else i
                r = pl.ds(pl.multiple_of(n * c, c), c)
                cs, sn = cos_ref[r, qk], sin_ref[r, qk]
                qh = ql_ref[r, qk] * cs + qs_ref[r, qk] * sn
                kh = (kl_ref[r, qk] * cs + ks_ref[r, qk] * sn) * k_scale
                out, state = step(qh, kh, vl_ref[r, vv], state, dmat, xi, zeta, cdm)
                ol_ref[r, vv] = ol_ref[r, vv] + out if backward else out
                return state

            lax.fori_loop(0, n_lat, lat_chunk, state)


def _retention(p_lat, p_ctx, log_f, log_b, cos, sin, batch):
    s = p_lat.shape[0] // batch
    c = p_ctx.shape[0] // batch
    qw = RET_Q_W
    swap0 = EV_COLS // qw
    smem = pl.BlockSpec(memory_space=pltpu.SMEM)
    lat = lambda width, j: pl.BlockSpec((s, width), lambda b: (b, j))
    ctx = lambda width, j: pl.BlockSpec((c, width), lambda b: (b, j))
    return pl.pallas_call(
        _retention_kernel,
        grid=(batch,),
        in_specs=[smem, smem, lat(qw, 0), lat(qw, 1), lat(RET_V_W, 1), lat(qw, swap0), lat(qw, swap0 + 1),
                  ctx(qw, 0), ctx(qw, 1), ctx(RET_V_W, 1), _full((s, qw)), _full((s, qw))],
        out_specs=[lat(RET_V_W, 0), ctx(RET_V_W, 0)],
        out_shape=[jax.ShapeDtypeStruct((batch * s, RET_V_W), F32), jax.ShapeDtypeStruct((batch * c, RET_V_W), F32)],
        compiler_params=pltpu.CompilerParams(vmem_limit_bytes=VMEM_MIXER_LIMIT),
    )(log_f, log_b, p_lat, p_lat, p_lat, p_lat, p_lat, p_ctx, p_ctx, p_ctx, cos, sin)


def _even_out_kernel(h_ref, gate_ref, ret_ref, g_ref, gb_ref, gc_ref, x_ref, gcp_ref, xp_ref, gcn_ref, xn_ref,
                     cw_ref, w_ref, o_ref, *, seq_blocks):
    i = pl.program_id(0)
    tm = h_ref.shape[0]
    u = gc_ref[...] * x_ref[...]
    seq_pos = i % seq_blocks
    keep_prev = jnp.where(seq_pos == 0, 0.0, 1.0)
    keep_next = jnp.where(seq_pos == seq_blocks - 1, 0.0, 1.0)
    u_prev = gcp_ref[7:8, :] * xp_ref[7:8, :] * keep_prev
    u_next = gcn_ref[0:1, :] * xn_ref[0:1, :] * keep_next
    rid = lax.broadcasted_iota(jnp.int32, u.shape, 0)
    up = jnp.where(rid == 0, u_prev, pltpu.roll(u, 1, 0))
    un = jnp.where(rid == tm - 1, u_next, pltpu.roll(u, tm - 1, 0))
    conv = cw_ref[0:1, :] * up + cw_ref[1:2, :] * u + cw_ref[2:3, :] * un
    parts = []
    for hh in range(RET_HEADS):
        lanes = slice(hh * RET_V_DIM, (hh + 1) * RET_V_DIM)
        r = ret_ref[:, lanes]
        g = g_ref[:, lanes]
        parts.append(r * lax.rsqrt(jnp.mean(r * r, axis=-1, keepdims=True) + EPS) * (g / (1.0 + jnp.exp(-g))))
    y = jnp.concatenate(parts + [gb_ref[...] * conv], axis=1)
    o_ref[...] = h_ref[...] + gate_ref[0] * _mm(y, w_ref[...])


def _even_output(h, gate, ret, p, conv_w, w_out, seq_len, rows_per_mod):
    n, d = h.shape
    tm = ROW_TILE
    cw = CONV_CH
    tiles = tm // 8
    last_tile = n // 8 - 1
    rows = lambda width, j: pl.BlockSpec((tm, width), lambda i: (i, j))
    prev = lambda j: pl.BlockSpec((8, cw), lambda i: (jnp.maximum(i * tiles - 1, 0), j))
    nxt = lambda j: pl.BlockSpec((8, cw), lambda i: (jnp.minimum((i + 1) * tiles, last_tile), j))
    mod = pl.BlockSpec((1, 1, d), lambda i: ((i * tm) // rows_per_mod, 0, 0))
    return pl.pallas_call(
        functools.partial(_even_out_kernel, seq_blocks=seq_len // tm),
        grid=(n // tm,),
        in_specs=[rows(d, 0), mod, rows(RET_V_W, 0), rows(cw, 2), rows(cw, 3), rows(cw, 4), rows(cw, 5),
                  prev(4), prev(5), nxt(4), nxt(5), _full((CONV_K, cw)), _full((d, d))],
        out_specs=rows(d, 0),
        out_shape=jax.ShapeDtypeStruct((n, d), F32),
    )(h, gate, ret, p, p, p, p, p, p, p, p, conv_w, w_out.astype(BF))


def _attn_kernel(sink_ref, h_ref, gate_ref, q_ref, qs_ref, kp_ref, kc_ref, kn_ref, ksp_ref, ksc_ref, ksn_ref,
                 vp_ref, vc_ref, vn_ref, kx_ref, vx_ref, cq_ref, sq_ref, ckp_ref, ckc_ref, ckn_ref,
                 skp_ref, skc_ref, skn_ref, w_ref, o_ref, *, n_lat):
    j = pl.program_id(1)
    blk = ATT_BLOCK
    span = blk + 2 * WINDOW
    n_keys = span + kx_ref.shape[0]
    scale = ATT_HEAD_DIM ** -0.5
    q = (q_ref[...] * cq_ref[...] + qs_ref[...] * sq_ref[...]) * scale
    keys = jnp.concatenate([kp_ref[...] * ckp_ref[...] + ksp_ref[...] * skp_ref[...],
                            kc_ref[...] * ckc_ref[...] + ksc_ref[...] * skc_ref[...],
                            kn_ref[...] * ckn_ref[...] + ksn_ref[...] * skn_ref[...],
                            kx_ref[...]], axis=0).astype(BF)
    vals = jnp.concatenate([vp_ref[...], vc_ref[...], vn_ref[...], vx_ref[...]], axis=0).astype(BF)
    qpos = lax.broadcasted_iota(jnp.int32, (blk, n_keys), 0)
    r = lax.broadcasted_iota(jnp.int32, (blk, n_keys), 1)
    key_pos = (j - 1) * blk + r
    in_band = (r >= qpos) & (r <= qpos + 2 * WINDOW) & (key_pos >= 0) & (key_pos < n_lat)
    valid = jnp.concatenate([in_band | (r >= span)] * ATT_GROUP, axis=0)
    heads = [None] * ATT_HEADS
    for kh in range(ATT_KV_HEADS):
        kv = slice(kh * ATT_HEAD_DIM, (kh + 1) * ATT_HEAD_DIM)
        qg = jnp.concatenate([q[:, (kh * ATT_GROUP + g) * ATT_HEAD_DIM:(kh * ATT_GROUP + g + 1) * ATT_HEAD_DIM]
                              for g in range(ATT_GROUP)], axis=0)
        s = lax.dot_general(qg.astype(BF), keys[:, kv], NT, preferred_element_type=F32)
        s = jnp.where(valid, s, NEG_INF)
        sink = jnp.concatenate([jnp.full((blk, 1), sink_ref[kh * ATT_GROUP + g], F32) for g in range(ATT_GROUP)], axis=0)
        m = jnp.maximum(jnp.max(s, axis=-1, keepdims=True), sink)
        p = jnp.exp(s - m)
        den = jnp.sum(p, axis=-1, keepdims=True) + jnp.exp(sink - m)
        o = _mm(p, vals[:, kv]) / den
        for g in range(ATT_GROUP):
            heads[kh * ATT_GROUP + g] = o[g * blk:(g + 1) * blk]
    o_ref[...] = h_ref[...] + gate_ref[0] * _mm(jnp.concatenate(heads, axis=1), w_ref[...])


def _attention(h, gate, p_lat, p_ctx, sinks, cos, sin, w_out, batch):
    n, d = h.shape
    s = n // batch
    c = p_ctx.shape[0] // batch
    blk = ATT_BLOCK
    nb = s // blk
    kw = ATT_KV_W
    k0 = 2 * ATT_Q_W // kw
    row = lambda width, col: pl.BlockSpec((blk, width), lambda b, j: (b * nb + j, col))
    prv = lambda width, col: pl.BlockSpec((blk, width), lambda b, j: (b * nb + jnp.maximum(j - 1, 0), col))
    nxt = lambda width, col: pl.BlockSpec((blk, width), lambda b, j: (b * nb + jnp.minimum(j + 1, nb - 1), col))
    tab = lambda width: pl.BlockSpec((blk, width), lambda b, j: (j, 0))
    tab_p = lambda width: pl.BlockSpec((blk, width), lambda b, j: (jnp.maximum(j - 1, 0), 0))
    tab_n = lambda width: pl.BlockSpec((blk, width), lambda b, j: (jnp.minimum(j + 1, nb - 1), 0))
    ctx = lambda col: pl.BlockSpec((c, kw), lambda b, j: (b, col))
    mod = pl.BlockSpec((1, 1, d), lambda b, j: (b, 0, 0))
    return pl.pallas_call(
        functools.partial(_attn_kernel, n_lat=s),
        grid=(batch, nb),
        in_specs=[pl.BlockSpec(memory_space=pltpu.SMEM), row(d, 0), mod, row(ATT_Q_W, 0), row(ATT_Q_W, 1),
                  prv(kw, k0), row(kw, k0), nxt(kw, k0), prv(kw, k0 + 1), row(kw, k0 + 1), nxt(kw, k0 + 1),
                  prv(kw, k0 + 2), row(kw, k0 + 2), nxt(kw, k0 + 2), ctx(0), ctx(1),
                  tab(ATT_Q_W), tab(ATT_Q_W), tab_p(kw), tab(kw), tab_n(kw), tab_p(kw), tab(kw), tab_n(kw),
                  pl.BlockSpec((d, d), lambda b, j: (0, 0))],
        out_specs=row(d, 0),
        out_shape=jax.ShapeDtypeStruct((n, d), F32),
    )(sinks, h, gate, p_lat, p_lat, p_lat, p_lat, p_lat, p_lat, p_lat, p_lat, p_lat, p_lat, p_lat, p_ctx, p_ctx,
      cos, sin, cos, cos, cos, sin, sin, sin, w_out.astype(BF))


def _pack_bf16_table(tab):
    e = tab.shape[0]
    bits = lax.bitcast_convert_type(tab.astype(BF), jnp.uint16).astype(jnp.uint32)
    word = (bits[:, HALF_D:] << 16) | bits[:, :HALF_D]
    return lax.bitcast_convert_type(word, jnp.int32).reshape(e * SLAB, 128)


def _unpack_words(words):
    lo = lax.bitcast_convert_type(words << 16, F32)
    hi = lax.bitcast_convert_type(words & BF16_HI_MASK, F32)
    return lo, hi


def _extract_top(s, order, n):
    vals, ids = [], []
    for _ in range(n):
        m = jnp.max(s, axis=0, keepdims=True)
        first = jnp.min(jnp.where(s == m, order, ID_BIG), axis=0, keepdims=True)
        vals.append(m)
        ids.append(first)
        s = jnp.where(order == first, -jnp.inf, s)
    return vals, ids


def _merge_sort_network(lo, hi):
    def merge(lo, hi, r):
        step = 2 * r
        if step < hi - lo:
            yield from merge(lo, hi, step)
            yield from merge(lo + r, hi, step)
            yield from [(i, i + r) for i in range(lo + r, hi - r, step)]
        else:
            yield (lo, lo + r)
    if hi > lo:
        mid = lo + (hi - lo) // 2
        yield from _merge_sort_network(lo, mid)
        yield from _merge_sort_network(mid + 1, hi)
        yield from merge(lo, hi, 1)


def _top_of_key_rows(s, key_id, n):
    depth = s.shape[0] // 8
    v = [s[8 * j:8 * j + 8, :] for j in range(depth)]
    ids = [key_id[8 * j:8 * j + 8, :] for j in range(depth)]
    for a, b in _merge_sort_network(0, depth - 1):
        a_first = (v[a] > v[b]) | ((v[a] == v[b]) & (ids[a] < ids[b]))
        v[a], v[b] = jnp.maximum(v[a], v[b]), jnp.minimum(v[a], v[b])
        ids[a], ids[b] = jnp.where(a_first, ids[a], ids[b]), jnp.where(a_first, ids[b], ids[a])
    out_v, out_i = [], []
    for r in range(n):
        live = min(depth, n - r)
        m = jnp.max(v[0], axis=0, keepdims=True)
        first = jnp.min(jnp.where(v[0] == m, ids[0], ID_BIG), axis=0, keepdims=True)
        out_v.append(m)
        out_i.append(first)
        if r + 1 < n:
            popped = ids[0] == first
            for j in range(live - 1):
                v[j] = jnp.where(popped, v[j + 1], v[j])
                ids[j] = jnp.where(popped, ids[j + 1], ids[j])
            v[live - 1] = jnp.where(popped, -jnp.inf, v[live - 1])
    return out_v, out_i


def _route_kernel(h_ref, g_ref, sh_ref, sc_ref, wq_ref, k1_ref, k2_ref, f_ref, idx_ref, gate_ref,
                  v1_ref, v2_ref, i1_ref, i2_ref, et_ref, gt_ref):
    tb = h_ref.shape[0]
    f = _norm_mod(h_ref[...], g_ref[...], sh_ref[0], sc_ref[0])
    f_ref[...] = f
    qb = _mm(f, wq_ref[...]).astype(BF)
    key_id = lax.broadcasted_iota(jnp.int32, (PEER_N_KEYS, tb), 0).astype(F32)
    r8 = lax.broadcasted_iota(jnp.int32, (8, tb), 0).astype(F32)
    flat = jnp.concatenate([r8 * 16, (r8 + 8) * 16, r8 * 16 + 1, r8 + 8, r8, r8 + 16, r8 + 32, r8 + 48, r8 + 64], axis=0)
    for h in range(PEER_HEADS):
        qh = qb[:, h * PEER_D_KEY:(h + 1) * PEER_D_KEY]
        s1 = lax.dot_general(k1_ref[h], qh, NT, preferred_element_type=F32)
        s2 = lax.dot_general(k2_ref[h], qh, NT, preferred_element_type=F32)
        for s, v_ref, i_ref in ((s1, v1_ref, i1_ref), (s2, v2_ref, i2_ref)):
            vals, ids = _top_of_key_rows(s, key_id, PEER_TOPK)
            for k in range(PEER_TOPK):
                v_ref[k:k + 1, :] = vals[k]
                i_ref[k:k + 1, :] = ids[k]
        v1a, v1b, v2a, v2b = v1_ref[0:8, :], v1_ref[8:16, :], v2_ref[0:8, :], v2_ref[8:16, :]
        i1a, i1b, i2a, i2b = i1_ref[0:8, :], i1_ref[8:16, :], i2_ref[0:8, :], i2_ref[8:16, :]
        ninf = -jnp.inf
        cand = jnp.concatenate([
            v1a + v2a[0:1], v1b + v2a[0:1], v1a + v2a[1:2], v2b + v1a[0:1],
            jnp.where(r8 >= 2, v2a + v1a[0:1], ninf),
            jnp.where(r8 >= 2, v2a + v1a[1:2], ninf),
            jnp.where((r8 >= 2) & (r8 <= 4), v2a + v1a[2:3], ninf),
            jnp.where((r8 >= 2) & (r8 <= 3), v2a + v1a[3:4], ninf),
            jnp.where(r8 == 2, v2a + v1a[4:5], ninf)], axis=0)
        expert = jnp.concatenate([
            i1a * PEER_N_KEYS + i2a[0:1], i1b * PEER_N_KEYS + i2a[0:1], i1a * PEER_N_KEYS + i2a[1:2],
            i1a[0:1] * PEER_N_KEYS + i2b,
            i1a[0:1] * PEER_N_KEYS + i2a, i1a[1:2] * PEER_N_KEYS + i2a, i1a[2:3] * PEER_N_KEYS + i2a,
            i1a[3:4] * PEER_N_KEYS + i2a, i1a[4:5] * PEER_N_KEYS + i2a], axis=0)
        cs, picks = _extract_top(cand, flat, PEER_TOPK)
        ex = [jnp.exp(c - cs[0]) for c in cs]
        den = ex[0]
        for e in ex[1:]:
            den = den + e
        for k in range(PEER_TOPK):
            row = h * PEER_TOPK + k
            pick = jnp.max(jnp.where(flat == picks[k], expert, -1.0), axis=0, keepdims=True)
            et_ref[row:row + 1, :] = (pick * SLAB).astype(jnp.int32)
            gt_ref[row:row + 1, :] = ex[k] / den
    idx_ref[...] = et_ref[...].T
    gate_ref[...] = gt_ref[...].T


def _peer_route(h, gain, shift, scale, rows_per_mod, wq, k1, k2):
    t, d = h.shape
    tb = ROUTE_TOKENS
    mod = pl.BlockSpec((1, 1, d), lambda i: ((i * tb) // rows_per_mod, 0, 0))
    slots = pl.BlockSpec((tb, N_SLOTS), lambda i: (i, 0))
    return pl.pallas_call(
        _route_kernel,
        grid=(t // tb,),
        in_specs=[pl.BlockSpec((tb, d), lambda i: (i, 0)), _full((1, d)), mod, mod,
                  _full((d, PEER_HEADS * PEER_D_KEY)), _full(k1.shape), _full(k2.shape)],
        out_specs=[pl.BlockSpec((tb, d), lambda i: (i, 0)), slots, slots],
        out_shape=[jax.ShapeDtypeStruct((t, d), F32), jax.ShapeDtypeStruct((t, N_SLOTS), jnp.int32),
                   jax.ShapeDtypeStruct((t, N_SLOTS), F32)],
        scratch_shapes=[pltpu.VMEM((PEER_TOPK, tb), F32), pltpu.VMEM((PEER_TOPK, tb), F32),
                        pltpu.VMEM((PEER_TOPK, tb), F32), pltpu.VMEM((PEER_TOPK, tb), F32),
                        pltpu.VMEM((N_SLOTS, tb), jnp.int32), pltpu.VMEM((N_SLOTS, tb), F32)],
    )(h, gain.reshape(1, d), shift, scale, wq, k1, k2)


def _load_slabs(tab_ref, slot_idx, k):
    return jnp.concatenate([tab_ref[pl.ds(pl.multiple_of(slot_idx[k + j], SLAB), SLAB), :] for j in range(PAIR)], axis=0)


def _peer_u_kernel(idx_ref, x_ref, gate_ref, tab_ref, w_ref, p_ref, r_ref):
    tb = x_ref.shape[0]
    rows = N_SLOTS * SLAB

    ones = jnp.ones((8, 128), BF)
    for t in range(tb):
        x = x_ref[t]
        xlo = jnp.concatenate([x[0:SLAB]] * PAIR, axis=0)
        xhi = jnp.concatenate([x[SLAB:2 * SLAB]] * PAIR, axis=0)
        slot_idx = idx_ref.at[t]
        for k in range(0, N_SLOTS, PAIR):
            lo, hi = _unpack_words(_load_slabs(tab_ref, slot_idx, k))
            p_ref[pl.ds(t * rows + k * SLAB, PAIR * SLAB), :] = lo * xlo + hi * xhi
        part = p_ref[pl.ds(t * rows, N_SLOTS, stride=SLAB), :]
        for s in range(1, SLAB):
            part = part + p_ref[pl.ds(t * rows + s, N_SLOTS, stride=SLAB), :]
        hi = part.astype(BF)
        lo = (part - hi.astype(F32)).astype(BF)
        r_ref[t:t + 1, :] = (lax.dot_general(ones, hi, NT, preferred_element_type=F32)
                             + lax.dot_general(ones, lo, NT, preferred_element_type=F32))[0:1]
    r = r_ref[...]
    w_ref[...] = 0.5 * r * (1.0 + lax.erf(r * SQRT_HALF)) * gate_ref[...]


def _peer_v_kernel(idx_ref, w_ref, tab_ref, y_ref, wrep_ref):
    tb = y_ref.shape[0]
    n_acc = 4
    upper = lax.broadcasted_iota(jnp.int32, (PAIR * SLAB, 128), 0) >= SLAB
    for t in range(tb):
        wrep_ref[t] = jnp.broadcast_to(w_ref[t:t + 1, :], (N_SLOTS, 128)).T

    for t in range(tb):
        acc_lo = [jnp.zeros((PAIR * SLAB, 128), F32) for _ in range(n_acc)]
        acc_hi = [jnp.zeros((PAIR * SLAB, 128), F32) for _ in range(n_acc)]
        slot_idx = idx_ref.at[t]
        for k in range(0, N_SLOTS, PAIR):
            lo, hi = _unpack_words(_load_slabs(tab_ref, slot_idx, k))
            w = jnp.where(upper, wrep_ref[t, pl.ds(k + 1, 1), :], wrep_ref[t, pl.ds(k, 1), :])
            j = (k // PAIR) % n_acc
            acc_lo[j] = acc_lo[j] + w * lo
            acc_hi[j] = acc_hi[j] + w * hi
        lo = (acc_lo[0] + acc_lo[1]) + (acc_lo[2] + acc_lo[3])
        hi = (acc_hi[0] + acc_hi[1]) + (acc_hi[2] + acc_hi[3])
        y_ref[t, 0:SLAB, :] = lo[0:SLAB] + lo[SLAB:2 * SLAB]
        y_ref[t, SLAB:2 * SLAB, :] = hi[0:SLAB] + hi[SLAB:2 * SLAB]


def _peer_experts(f, idx, gate, u_words, v_words):
    t, d = f.shape
    tb = PEER_TOKENS
    smem = pl.BlockSpec((tb, N_SLOTS), lambda i: (i, 0), memory_space=pltpu.SMEM)
    slots = pl.BlockSpec((tb, N_SLOTS), lambda i: (i, 0))
    resident = pl.BlockSpec(memory_space=pltpu.VMEM)
    rows3 = pl.BlockSpec((tb, 2 * SLAB, 128), lambda i: (i, 0, 0))
    params = pltpu.CompilerParams(vmem_limit_bytes=VMEM_TABLE_LIMIT)
    w = pl.pallas_call(
        _peer_u_kernel,
        grid=(t // tb,),
        in_specs=[smem, rows3, slots, resident],
        out_specs=slots,
        out_shape=jax.ShapeDtypeStruct((t, N_SLOTS), F32),
        scratch_shapes=[pltpu.VMEM((tb * N_SLOTS * SLAB, 128), F32), pltpu.VMEM((tb, N_SLOTS), F32)],
        compiler_params=params,
    )(idx, f.reshape(t, 2 * SLAB, 128), gate, u_words)
    y = pl.pallas_call(
        _peer_v_kernel,
        grid=(t // tb,),
        in_specs=[smem, slots, resident],
        out_specs=rows3,
        out_shape=jax.ShapeDtypeStruct((t, 2 * SLAB, 128), F32),
        scratch_shapes=[pltpu.VMEM((tb, N_SLOTS, 128), F32)],
        compiler_params=params,
    )(idx, w, v_words)
    return y.reshape(t, d)


class _PeerWeights:
    def __init__(self, wq, keys1, keys2, u_tab, v_tab):
        half = PEER_D_KEY // 2
        self.wq = wq.astype(BF)
        self.k1 = jnp.pad(keys1, ((0, 0), (0, 0), (0, half))).astype(BF)
        self.k2 = jnp.pad(keys2, ((0, 0), (0, 0), (half, 0))).astype(BF)
        self.u = _pack_bf16_table(u_tab)
        self.v = _pack_bf16_table(v_tab)


def _peer_ffn(h, gain, shift, scale, rows_per_mod, pw):
    f, idx, gate = _peer_route(h, gain, shift, scale, rows_per_mod, pw.wq, pw.k1, pw.k2)
    return _peer_experts(f, idx, gate, pw.u, pw.v)


def _final_kernel(h_ref, y_ref, gate_ref, g_ref, o_ref):
    x = h_ref[...] + gate_ref[0] * y_ref[...]
    o_ref[...] = x * lax.rsqrt(jnp.mean(x * x, axis=-1, keepdims=True) + EPS) * g_ref[...]


def _final_norm(h, y, gate, gain, rows_per_mod):
    n, d = h.shape
    tm = ROW_TILE
    rows = pl.BlockSpec((tm, d), lambda i: (i, 0))
    mod = pl.BlockSpec((1, 1, d), lambda i: ((i * tm) // rows_per_mod, 0, 0))
    return pl.pallas_call(
        _final_kernel, grid=(n // tm,), in_specs=[rows, rows, mod, _full((1, d))], out_specs=rows,
        out_shape=jax.ShapeDtypeStruct((n, d), F32),
    )(h, y, gate, gain.reshape(1, d))


def kernel(x, c, ctx, c_ctx, ada_w, ada_b, mix_norm_g, ffn_norm_g, ev_w_in, ev_w_out,
           ret_decay_logit_f, ret_decay_logit_b, conv_w, od_w_in, od_w_out, attn_sinks,
           peer_wq, peer_keys1, peer_keys2, peer_u, peer_v, final_norm_g):
    batch, s, d = x.shape
    n_ctx = ctx.shape[1]
    assert DEPTH == 2 and s % ROW_TILE == 0 and n_ctx % ROW_TILE == 0 and batch + 1 <= ADA_ROWS
    h_lat = x.reshape(batch * s, d)
    h_ctx = ctx.reshape(batch * n_ctx, d)
    c_rows = jnp.zeros((ADA_ROWS, d), F32).at[:batch].set(c).at[batch].set(c_ctx)

    def modulation(layer):
        mod = _ada_modulation(c_rows, ada_w[layer], ada_b[layer])
        lat = [m.reshape(batch, 1, d) for m in jnp.split(mod[:batch], N_ADA, axis=-1)]
        cx = [m.reshape(1, 1, d) for m in jnp.split(mod[batch:batch + 1], N_ADA, axis=-1)]
        return lat, cx

    (sh1, sc1, g1, sh2, sc2, g2), (csh1, csc1, cg1, csh2, csc2, cg2) = modulation(0)
    w_in = ev_w_in[0]
    swap = _swap_columns(RET_HEADS, RET_QK_DIM)
    w0 = jnp.concatenate([w_in, w_in[:, :RET_Q_W][:, swap], w_in[:, RET_Q_W:2 * RET_Q_W][:, swap]], axis=1)
    p_lat = _in_projection(h_lat, mix_norm_g[0], sh1, sc1, w0, s)
    p_ctx = _in_projection(h_ctx, mix_norm_g[0], csh1, csc1, w0, batch * n_ctx)
    log_f = jax.nn.log_sigmoid(ret_decay_logit_f[0].astype(F32))
    log_b = jax.nn.log_sigmoid(ret_decay_logit_b[0].astype(F32))
    cos, sin = _rope_tables(s, RET_QK_DIM, RET_HEADS)
    ret_lat, ret_ctx = _retention(p_lat, p_ctx, log_f, log_b, cos, sin, batch)
    h_lat = _even_output(h_lat, g1, ret_lat, p_lat, conv_w[0], ev_w_out[0], s, s)
    h_ctx = _even_output(h_ctx, cg1, ret_ctx, p_ctx, conv_w[0], ev_w_out[0], n_ctx, batch * n_ctx)
    pw = _PeerWeights(peer_wq[0], peer_keys1[0], peer_keys2[0], peer_u[0], peer_v[0])
    y_lat = _peer_ffn(h_lat, ffn_norm_g[0], sh2, sc2, s, pw)
    y_ctx = _peer_ffn(h_ctx, ffn_norm_g[0], csh2, csc2, batch * n_ctx, pw)

    (sh1, sc1, g1, sh2, sc2, g2b), (csh1, csc1, _, _, _, _) = modulation(1)
    w_in = od_w_in[0]
    wq_cols, wk_cols, wv_cols = w_in[:, :ATT_Q_W], w_in[:, ATT_Q_W:ATT_Q_W + ATT_KV_W], w_in[:, ATT_Q_W + ATT_KV_W:]
    w1 = jnp.concatenate([wq_cols, wq_cols[:, _swap_columns(ATT_HEADS, ATT_HEAD_DIM)], wk_cols,
                          wk_cols[:, _swap_columns(ATT_KV_HEADS, ATT_HEAD_DIM)], wv_cols], axis=1)
    h_lat, p_lat = _in_projection(h_lat, mix_norm_g[1], sh1, sc1, w1, s, add=(y_lat, g2), emit_h=True)
    p_ctx = _in_projection(h_ctx, mix_norm_g[1], csh1, csc1, w_in[:, ATT_Q_W:], batch * n_ctx, add=(y_ctx, cg2))
    cos, sin = _rope_tables(s, ATT_HEAD_DIM, ATT_HEADS)
    h_lat = _attention(h_lat, g1, p_lat, p_ctx, attn_sinks[0].astype(F32), cos, sin, od_w_out[0], batch)
    pw = _PeerWeights(peer_wq[1], peer_keys1[1], peer_keys2[1], peer_u[1], peer_v[1])
    y_lat = _peer_ffn(h_lat, ffn_norm_g[1], sh2, sc2, s, pw)
    out = _final_norm(h_lat, y_lat, g2b, final_norm_g, s)
    return out.reshape(batch, s, d)
```

```python
import functools

import numpy as np
import jax
import jax.numpy as jnp
from jax import lax
from jax.experimental import pallas as pl
from jax.experimental.pallas import tpu as pltpu

D_MODEL = 1024
DEPTH = 2
GRID_W = 64
EPS = 1e-6
ROPE_BASE = 10000.0
NEG_INF = -1e30
N_ADA = 6
RET_HEADS = 4
RET_V_DIM = D_MODEL // (2 * RET_HEADS)
RET_QK_DIM = RET_V_DIM // 2
RET_CHUNK = 128
CONV_CH = D_MODEL // 2
CONV_K = 3
RET_Q_W = RET_HEADS * RET_QK_DIM
RET_V_W = RET_HEADS * RET_V_DIM
EV_COLS = 2 * RET_Q_W + 2 * RET_V_W + 3 * CONV_CH
ATT_HEADS = 16
ATT_HEAD_DIM = D_MODEL // ATT_HEADS
ATT_KV_HEADS = 4
ATT_GROUP = ATT_HEADS // ATT_KV_HEADS
WINDOW = 128
ATT_BLOCK = 128
ATT_Q_W = ATT_HEADS * ATT_HEAD_DIM
ATT_KV_W = ATT_KV_HEADS * ATT_HEAD_DIM
PEER_HEADS = 8
PEER_N_KEYS = 128
PEER_D_KEY = 128
PEER_TOPK = 16
ROUTE_TOKENS = 128
PEER_TOKENS = 32
N_SLOTS = PEER_HEADS * PEER_TOPK
HALF_D = D_MODEL // 2
SLAB = HALF_D // 128
PAIR = 8 // SLAB
BF16_HI_MASK = -65536
VMEM_TABLE_LIMIT = 48 * 1024 * 1024
VMEM_MIXER_LIMIT = 40 * 1024 * 1024
ID_BIG = 1e9
SQRT_HALF = 0.7071067811865476
ROW_TILE = 256
ADA_ROWS = 40

NT = (((1,), (1,)), ((), ()))
TN = (((0,), (0,)), ((), ()))
BF = jnp.bfloat16
F32 = jnp.float32


def _full(shape):
    return pl.BlockSpec(shape, lambda *_: (0,) * len(shape))


def _mm(a, b):
    return jnp.dot(a.astype(BF), b.astype(BF), preferred_element_type=F32)


def _rope_tables(n_tok, head_dim, n_heads):
    n_rows = n_tok // GRID_W
    rows = jnp.broadcast_to(jnp.arange(n_rows, dtype=F32)[:, None], (n_rows, GRID_W)).reshape(-1)
    cols = jnp.broadcast_to(jnp.arange(GRID_W, dtype=F32)[None, :], (n_rows, GRID_W)).reshape(-1)
    n_freq = head_dim // 4
    inv_freq = ROPE_BASE ** (-jnp.arange(n_freq, dtype=F32) / n_freq)
    ar, ac = rows[:, None] * inv_freq, cols[:, None] * inv_freq
    cos = jnp.concatenate([jnp.cos(ar), jnp.cos(ar), jnp.cos(ac), jnp.cos(ac)], axis=-1)
    sin = jnp.concatenate([-jnp.sin(ar), jnp.sin(ar), -jnp.sin(ac), jnp.sin(ac)], axis=-1)
    return jnp.tile(cos, (1, n_heads)), jnp.tile(sin, (1, n_heads))


def _swap_columns(n_heads, head_dim):
    q = head_dim // 4
    base = np.concatenate([np.arange(q, 2 * q), np.arange(0, q), np.arange(3 * q, 4 * q), np.arange(2 * q, 3 * q)])
    return (np.arange(n_heads)[:, None] * head_dim + base[None, :]).reshape(-1)


def _ada_kernel(c_ref, w_ref, b_ref, o_ref):
    c = c_ref[...]
    s = c / (1.0 + jnp.exp(-c))
    o_ref[...] = _mm(s, w_ref[...]) + b_ref[...]


def _ada_modulation(c_rows, w, b):
    d, f = w.shape
    tn = f // 4
    return pl.pallas_call(
        _ada_kernel,
        grid=(f // tn,),
        in_specs=[_full((ADA_ROWS, d)), pl.BlockSpec((d, tn), lambda j: (0, j)), pl.BlockSpec((1, tn), lambda j: (0, j))],
        out_specs=pl.BlockSpec((ADA_ROWS, tn), lambda j: (0, j)),
        out_shape=jax.ShapeDtypeStruct((ADA_ROWS, f), F32),
    )(c_rows, w.astype(BF), b.reshape(1, f))


def _norm_mod(x, gain, shift, scale):
    n = x * lax.rsqrt(jnp.mean(x * x, axis=-1, keepdims=True) + EPS) * gain
    return n * (1.0 + scale) + shift


def _proj_kernel(*refs, has_add, emit_h):
    refs = list(refs)
    h_ref = refs.pop(0)
    x = h_ref[...]
    if has_add:
        y_ref, gt_ref = refs.pop(0), refs.pop(0)
        x = x + gt_ref[0] * y_ref[...]
    g_ref, sh_ref, sc_ref, w_ref = refs[:4]
    outs = refs[4:]
    if emit_h:
        outs.pop(0)[...] = x
    a = _norm_mod(x, g_ref[...], sh_ref[0], sc_ref[0])
    outs[0][...] = _mm(a, w_ref[...])


def _in_projection(h, gain, shift, scale, w, rows_per_mod, add=None, emit_h=False):
    n, d = h.shape
    f = w.shape[1]
    tm = ROW_TILE
    rows = pl.BlockSpec((tm, d), lambda i: (i, 0))
    mod = pl.BlockSpec((1, 1, d), lambda i: ((i * tm) // rows_per_mod, 0, 0))
    args, specs = [h], [rows]
    if add is not None:
        args += [add[0], add[1]]
        specs += [rows, mod]
    args += [gain.reshape(1, d), shift, scale, w.astype(BF)]
    specs += [_full((1, d)), mod, mod, _full((d, f))]
    out_shape = [jax.ShapeDtypeStruct((n, f), F32)]
    out_specs = [pl.BlockSpec((tm, f), lambda i: (i, 0))]
    if emit_h:
        out_shape.insert(0, jax.ShapeDtypeStruct((n, d), F32))
        out_specs.insert(0, rows)
    res = pl.pallas_call(
        functools.partial(_proj_kernel, has_add=add is not None, emit_h=emit_h),
        grid=(n // tm,), in_specs=specs, out_specs=out_specs, out_shape=out_shape,
        compiler_params=pltpu.CompilerParams(vmem_limit_bytes=VMEM_MIXER_LIMIT),
    )(*args)
    return res if emit_h else res[0]


def _retention_kernel(logf_ref, logb_ref, ql_ref, kl_ref, vl_ref, qs_ref, ks_ref, qc_ref, kc_ref, vc_ref,
                      cos_ref, sin_ref, ol_ref, oc_ref):
    c = RET_CHUNK
    n_lat = ql_ref.shape[0] // c
    n_ctx = qc_ref.shape[0] // c
    k_scale = RET_QK_DIM ** -0.5
    row = lax.broadcasted_iota(jnp.int32, (c, c), 0).astype(F32)
    col = lax.broadcasted_iota(jnp.int32, (c, c), 1).astype(F32)
    rowk = lax.broadcasted_iota(jnp.int32, (c, RET_QK_DIM), 0).astype(F32)

    def step(qh, kh, vh, state, dmat, xi, zeta, cdm):
        qb, vb = qh.astype(BF), vh.astype(BF)
        scores = lax.dot_general(qb, kh.astype(BF), NT, preferred_element_type=F32) * dmat
        out = _mm(scores, vb) + _mm(qb, state) * xi
        kv = lax.dot_general((kh * zeta).astype(BF), vb, TN, preferred_element_type=F32)
        return out, cdm * state + kv

    heads = range(RET_HEADS)
    qk = [slice(h * RET_QK_DIM, (h + 1) * RET_QK_DIM) for h in heads]
    vv = [slice(h * RET_V_DIM, (h + 1) * RET_V_DIM) for h in heads]
    for backward in (False, True):
        decay = []
        for h in heads:
            lg = (logb_ref if backward else logf_ref)[h]
            if backward:
                dmat = jnp.where(col >= row, jnp.exp(lg * (col - row)), 0.0)
                xi = jnp.exp(lg * (c - row))
                zeta = jnp.exp(lg * rowk)
            else:
                dmat = jnp.where(row >= col, jnp.exp(lg * (row - col)), 0.0)
                xi = jnp.exp(lg * (row + 1.0))
                zeta = jnp.exp(lg * (c - 1.0 - rowk))
            decay.append((dmat, xi, zeta, jnp.exp(jnp.full((RET_QK_DIM, RET_V_DIM), lg * c, F32))))
        states = [jnp.zeros((RET_QK_DIM, RET_V_DIM), F32) for _ in heads]
        for n in (range(n_ctx - 1, -1, -1) if backward else range(n_ctx)):
            r = slice(n * c, (n + 1) * c)
            for h in heads:
                out, states[h] = step(qc_ref[r, qk[h]], kc_ref[r, qk[h]] * k_scale, vc_ref[r, vv[h]], states[h], *decay[h])
                oc_ref[r, vv[h]] = oc_ref[r, vv[h]] + out if backward else out

        def lat_chunk(i, states):
            n = (n_lat - 1 - i) if backward else i
            r = pl.ds(pl.multiple_of(n * c, c), c)
            new_states = []
            for h in heads:
                cs, sn = cos_ref[r, qk[h]], sin_ref[r, qk[h]]
                qh = ql_ref[r, qk[h]] * cs + qs_ref[r, qk[h]] * sn
                kh = (kl_ref[r, qk[h]] * cs + ks_ref[r, qk[h]] * sn) * k_scale
                out, state = step(qh, kh, vl_ref[r, vv[h]], states[h], *decay[h])
                ol_ref[r, vv[h]] = ol_ref[r, vv[h]] + out if backward else out
                new_states.append(state)
            return tuple(new_states)

        lax.fori_loop(0, n_lat, lat_chunk, tuple(states))


def _retention(p_lat, p_ctx, log_f, log_b, cos, sin, batch):
    s = p_lat.shape[0] // batch
    c = p_ctx.shape[0] // batch
    qw = RET_Q_W
    swap0 = EV_COLS // qw
    smem = pl.BlockSpec(memory_space=pltpu.SMEM)
    lat = lambda width, j: pl.BlockSpec((s, width), lambda b: (b, j))
    ctx = lambda width, j: pl.BlockSpec((c, width), lambda b: (b, j))
    return pl.pallas_call(
        _retention_kernel,
        grid=(batch,),
        in_specs=[smem, smem, lat(qw, 0), lat(qw, 1), lat(RET_V_W, 1), lat(qw, swap0), lat(qw, swap0 + 1),
                  ctx(qw, 0), ctx(qw, 1), ctx(RET_V_W, 1), _full((s, qw)), _full((s, qw))],
        out_specs=[lat(RET_V_W, 0), ctx(RET_V_W, 0)],
        out_shape=[jax.ShapeDtypeStruct((batch * s, RET_V_W), F32), jax.ShapeDtypeStruct((batch * c, RET_V_W), F32)],
        compiler_params=pltpu.CompilerParams(vmem_limit_bytes=VMEM_MIXER_LIMIT),
    )(log_f, log_b, p_lat, p_lat, p_lat, p_lat, p_lat, p_ctx, p_ctx, p_ctx, cos, sin)


def _even_out_kernel(h_ref, gate_ref, ret_ref, g_ref, gb_ref, gc_ref, x_ref, gcp_ref, xp_ref, gcn_ref, xn_ref,
                     cw_ref, w_ref, o_ref, *, seq_blocks):
    i = pl.program_id(0)
    tm = h_ref.shape[0]
    u = gc_ref[...] * x_ref[...]
    seq_pos = i % seq_blocks
    keep_prev = jnp.where(seq_pos == 0, 0.0, 1.0)
    keep_next = jnp.where(seq_pos == seq_blocks - 1, 0.0, 1.0)
    u_prev = gcp_ref[7:8, :] * xp_ref[7:8, :] * keep_prev
    u_next = gcn_ref[0:1, :] * xn_ref[0:1, :] * keep_next
    rid = lax.broadcasted_iota(jnp.int32, u.shape, 0)
    up = jnp.where(rid == 0, u_prev, pltpu.roll(u, 1, 0))
    un = jnp.where(rid == tm - 1, u_next, pltpu.roll(u, tm - 1, 0))
    conv = cw_ref[0:1, :] * up + cw_ref[1:2, :] * u + cw_ref[2:3, :] * un
    parts = []
    for hh in range(RET_HEADS):
        lanes = slice(hh * RET_V_DIM, (hh + 1) * RET_V_DIM)
        r = ret_ref[:, lanes]
        g = g_ref[:, lanes]
        parts.append(r * lax.rsqrt(jnp.mean(r * r, axis=-1, keepdims=True) + EPS) * (g / (1.0 + jnp.exp(-g))))
    y = jnp.concatenate(parts + [gb_ref[...] * conv], axis=1)
    o_ref[...] = h_ref[...] + gate_ref[0] * _mm(y, w_ref[...])


def _even_output(h, gate, ret, p, conv_w, w_out, seq_len, rows_per_mod):
    n, d = h.shape
    tm = ROW_TILE
    cw = CONV_CH
    tiles = tm // 8
    last_tile = n // 8 - 1
    rows = lambda width, j: pl.BlockSpec((tm, width), lambda i: (i, j))
    prev = lambda j: pl.BlockSpec((8, cw), lambda i: (jnp.maximum(i * tiles - 1, 0), j))
    nxt = lambda j: pl.BlockSpec((8, cw), lambda i: (jnp.minimum((i + 1) * tiles, last_tile), j))
    mod = pl.BlockSpec((1, 1, d), lambda i: ((i * tm) // rows_per_mod, 0, 0))
    return pl.pallas_call(
        functools.partial(_even_out_kernel, seq_blocks=seq_len // tm),
        grid=(n // tm,),
        in_specs=[rows(d, 0), mod, rows(RET_V_W, 0), rows(cw, 2), rows(cw, 3), rows(cw, 4), rows(cw, 5),
                  prev(4), prev(5), nxt(4), nxt(5), _full((CONV_K, cw)), _full((d, d))],
        out_specs=rows(d, 0),
        out_shape=jax.ShapeDtypeStruct((n, d), F32),
    )(h, gate, ret, p, p, p, p, p, p, p, p, conv_w, w_out.astype(BF))


def _attn_kernel(sink_ref, h_ref, gate_ref, q_ref, qs_ref, kp_ref, kc_ref, kn_ref, ksp_ref, ksc_ref, ksn_ref,
                 vp_ref, vc_ref, vn_ref, kx_ref, vx_ref, cq_ref, sq_ref, ckp_ref, ckc_ref, ckn_ref,
                 skp_ref, skc_ref, skn_ref, w_ref, o_ref, *, n_lat):
    j = pl.program_id(1)
    blk = ATT_BLOCK
    span = blk + 2 * WINDOW
    n_keys = span + kx_ref.shape[0]
    scale = ATT_HEAD_DIM ** -0.5
    q = (q_ref[...] * cq_ref[...] + qs_ref[...] * sq_ref[...]) * scale
    keys = jnp.concatenate([kp_ref[...] * ckp_ref[...] + ksp_ref[...] * skp_ref[...],
                            kc_ref[...] * ckc_ref[...] + ksc_ref[...] * skc_ref[...],
                            kn_ref[...] * ckn_ref[...] + ksn_ref[...] * skn_ref[...],
                            kx_ref[...]], axis=0).astype(BF)
    vals = jnp.concatenate([vp_ref[...], vc_ref[...], vn_ref[...], vx_ref[...]], axis=0).astype(BF)
    qpos = lax.broadcasted_iota(jnp.int32, (blk, n_keys), 0)
    r = lax.broadcasted_iota(jnp.int32, (blk, n_keys), 1)
    key_pos = (j - 1) * blk + r
    in_band = (r >= qpos) & (r <= qpos + 2 * WINDOW) & (key_pos >= 0) & (key_pos < n_lat)
    valid = jnp.concatenate([in_band | (r >= span)] * ATT_GROUP, axis=0)
    heads = [None] * ATT_HEADS
    for kh in range(ATT_KV_HEADS):
        kv = slice(kh * ATT_HEAD_DIM, (kh + 1) * ATT_HEAD_DIM)
        qg = jnp.concatenate([q[:, (kh * ATT_GROUP + g) * ATT_HEAD_DIM:(kh * ATT_GROUP + g + 1) * ATT_HEAD_DIM]
                              for g in range(ATT_GROUP)], axis=0)
        s = lax.dot_general(qg.astype(BF), keys[:, kv], NT, preferred_element_type=F32)
        s = jnp.where(valid, s, NEG_INF)
        sink = jnp.concatenate([jnp.full((blk, 1), sink_ref[kh * ATT_GROUP + g], F32) for g in range(ATT_GROUP)], axis=0)
        m = jnp.maximum(jnp.max(s, axis=-1, keepdims=True), sink)
        p = jnp.exp(s - m)
        den = jnp.sum(p, axis=-1, keepdims=True) + jnp.exp(sink - m)
        o = _mm(p, vals[:, kv]) / den
        for g in range(ATT_GROUP):
            heads[kh * ATT_GROUP + g] = o[g * blk:(g + 1) * blk]
    o_ref[...] = h_ref[...] + gate_ref[0] * _mm(jnp.concatenate(heads, axis=1), w_ref[...])


def _attention(h, gate, p_lat, p_ctx, sinks, cos, sin, w_out, batch):
    n, d = h.shape
    s = n // batch
    c = p_ctx.shape[0] // batch
    blk = ATT_BLOCK
    nb = s // blk
    kw = ATT_KV_W
    k0 = 2 * ATT_Q_W // kw
    row = lambda width, col: pl.BlockSpec((blk, width), lambda b, j: (b * nb + j, col))
    prv = lambda width, col: pl.BlockSpec((blk, width), lambda b, j: (b * nb + jnp.maximum(j - 1, 0), col))
    nxt = lambda width, col: pl.BlockSpec((blk, width), lambda b, j: (b * nb + jnp.minimum(j + 1, nb - 1), col))
    tab = lambda width: pl.BlockSpec((blk, width), lambda b, j: (j, 0))
    tab_p = lambda width: pl.BlockSpec((blk, width), lambda b, j: (jnp.maximum(j - 1, 0), 0))
    tab_n = lambda width: pl.BlockSpec((blk, width), lambda b, j: (jnp.minimum(j + 1, nb - 1), 0))
    ctx = lambda col: pl.BlockSpec((c, kw), lambda b, j: (b, col))
    mod = pl.BlockSpec((1, 1, d), lambda b, j: (b, 0, 0))
    return pl.pallas_call(
        functools.partial(_attn_kernel, n_lat=s),
        grid=(batch, nb),
        in_specs=[pl.BlockSpec(memory_space=pltpu.SMEM), row(d, 0), mod, row(ATT_Q_W, 0), row(ATT_Q_W, 1),
                  prv(kw, k0), row(kw, k0), nxt(kw, k0), prv(kw, k0 + 1), row(kw, k0 + 1), nxt(kw, k0 + 1),
                  prv(kw, k0 + 2), row(kw, k0 + 2), nxt(kw, k0 + 2), ctx(0), ctx(1),
                  tab(ATT_Q_W), tab(ATT_Q_W), tab_p(kw), tab(kw), tab_n(kw), tab_p(kw), tab(kw), tab_n(kw),
                  pl.BlockSpec((d, d), lambda b, j: (0, 0))],
        out_specs=row(d, 0),
        out_shape=jax.ShapeDtypeStruct((n, d), F32),
    )(sinks, h, gate, p_lat, p_lat, p_lat, p_lat, p_lat, p_lat, p_lat, p_lat, p_lat, p_lat, p_lat, p_ctx, p_ctx,
      cos, sin, cos, cos, cos, sin, sin, sin, w_out.astype(BF))


def _pack_bf16_table(tab):
    e = tab.shape[0]
    bits = lax.bitcast_convert_type(tab.astype(BF), jnp.uint16).astype(jnp.uint32)
    word = (bits[:, HALF_D:] << 16) | bits[:, :HALF_D]
    return lax.bitcast_convert_type(word, jnp.int32).reshape(e * SLAB, 128)


def _unpack_words(words):
    lo = lax.bitcast_convert_type(words << 16, F32)
    hi = lax.bitcast_convert_type(words & BF16_HI_MASK, F32)
    return lo, hi


def _extract_top(s, order, n):
    vals, ids = [], []
    for _ in range(n):
        m = jnp.max(s, axis=0, keepdims=True)
        first = jnp.min(jnp.where(s == m, order, ID_BIG), axis=0, keepdims=True)
        vals.append(m)
        ids.append(first)
        s = jnp.where(order == first, -jnp.inf, s)
    return vals, ids


def _merge_sort_network(lo, hi):
    def merge(lo, hi, r):
        step = 2 * r
        if step < hi - lo:
            yield from merge(lo, hi, step)
            yield from merge(lo + r, hi, step)
            yield from [(i, i + r) for i in range(lo + r, hi - r, step)]
        else:
            yield (lo, lo + r)
    if hi > lo:
        mid = lo + (hi - lo) // 2
        yield from _merge_sort_network(lo, mid)
        yield from _merge_sort_network(mid + 1, hi)
        yield from merge(lo, hi, 1)


def _top_of_key_rows(s, key_id, n):
    depth = s.shape[0] // 8
    v = [s[8 * j:8 * j + 8, :] for j in range(depth)]
    ids = [key_id[8 * j:8 * j + 8, :] for j in range(depth)]
    for a, b in _merge_sort_network(0, depth - 1):
        a_first = (v[a] > v[b]) | ((v[a] == v[b]) & (ids[a] < ids[b]))
        v[a], v[b] = jnp.maximum(v[a], v[b]), jnp.minimum(v[a], v[b])
        ids[a], ids[b] = jnp.where(a_first, ids[a], ids[b]), jnp.where(a_first, ids[b], ids[a])
    out_v, out_i = [], []
    for r in range(n):
        live = min(depth, n - r)
        m = jnp.max(v[0], axis=0, keepdims=True)
        first = jnp.min(jnp.where(v[0] == m, ids[0], ID_BIG), axis=0, keepdims=True)
        out_v.append(m)
        out_i.append(first)
        if r + 1 < n:
            popped = ids[0] == first
            for j in range(live - 1):
                v[j] = jnp.where(popped, v[j + 1], v[j])
                ids[j] = jnp.where(popped, ids[j + 1], ids[j])
            v[live - 1] = jnp.where(popped, -jnp.inf, v[live - 1])
    return out_v, out_i


def _route_kernel(h_ref, g_ref, sh_ref, sc_ref, wq_ref, k1_ref, k2_ref, f_ref, idx_ref, gate_ref,
                  v1_ref, v2_ref, i1_ref, i2_ref, et_ref, gt_ref):
    tb = h_ref.shape[0]
    f = _norm_mod(h_ref[...], g_ref[...], sh_ref[0], sc_ref[0])
    for s in range(f_ref.shape[1]):
        f_ref[:, s, :] = f[:, s * 128:(s + 1) * 128]
    qb = _mm(f, wq_ref[...]).astype(BF)
    key_id = lax.broadcasted_iota(jnp.int32, (PEER_N_KEYS, tb), 0).astype(F32)
    r8 = lax.broadcasted_iota(jnp.int32, (8, tb), 0).astype(F32)
    flat = jnp.concatenate([r8 * 16, (r8 + 8) * 16, r8 * 16 + 1, r8 + 8, r8, r8 + 16, r8 + 32, r8 + 48, r8 + 64], axis=0)
    for h in range(PEER_HEADS):
        qh = qb[:, h * PEER_D_KEY:(h + 1) * PEER_D_KEY]
        s1 = lax.dot_general(k1_ref[h], qh, NT, preferred_element_type=F32)
        s2 = lax.dot_general(k2_ref[h], qh, NT, preferred_element_type=F32)
        for s, v_ref, i_ref in ((s1, v1_ref, i1_ref), (s2, v2_ref, i2_ref)):
            vals, ids = _top_of_key_rows(s, key_id, PEER_TOPK)
            for k in range(PEER_TOPK):
                v_ref[k:k + 1, :] = vals[k]
                i_ref[k:k + 1, :] = ids[k]
        v1a, v1b, v2a, v2b = v1_ref[0:8, :], v1_ref[8:16, :], v2_ref[0:8, :], v2_ref[8:16, :]
        i1a, i1b, i2a, i2b = i1_ref[0:8, :], i1_ref[8:16, :], i2_ref[0:8, :], i2_ref[8:16, :]
        ninf = -jnp.inf
        cand = jnp.concatenate([
            v1a + v2a[0:1], v1b + v2a[0:1], v1a + v2a[1:2], v2b + v1a[0:1],
            jnp.where(r8 >= 2, v2a + v1a[0:1], ninf),
            jnp.where(r8 >= 2, v2a + v1a[1:2], ninf),
            jnp.where((r8 >= 2) & (r8 <= 4), v2a + v1a[2:3], ninf),
            jnp.where((r8 >= 2) & (r8 <= 3), v2a + v1a[3:4], ninf),
            jnp.where(r8 == 2, v2a + v1a[4:5], ninf)], axis=0)
        expert = jnp.concatenate([
            i1a * PEER_N_KEYS + i2a[0:1], i1b * PEER_N_KEYS + i2a[0:1], i1a * PEER_N_KEYS + i2a[1:2],
            i1a[0:1] * PEER_N_KEYS + i2b,
            i1a[0:1] * PEER_N_KEYS + i2a, i1a[1:2] * PEER_N_KEYS + i2a, i1a[2:3] * PEER_N_KEYS + i2a,
            i1a[3:4] * PEER_N_KEYS + i2a, i1a[4:5] * PEER_N_KEYS + i2a], axis=0)
        cs, picks = _extract_top(cand, flat, PEER_TOPK)
        ex = [jnp.exp(c - cs[0]) for c in cs]
        den = ex[0]
        for e in ex[1:]:
            den = den + e
        for k in range(PEER_TOPK):
            row = h * PEER_TOPK + k
            pick = jnp.max(jnp.where(flat == picks[k], expert, -1.0), axis=0, keepdims=True)
            et_ref[row:row + 1, :] = (pick * SLAB).astype(jnp.int32)
            gt_ref[row:row + 1, :] = ex[k] / den
    idx_ref[...] = et_ref[...].T
    gate_ref[...] = gt_ref[...].T


def _peer_route(h, gain, shift, scale, rows_per_mod, wq, k1, k2):
    t, d = h.shape
    tb = ROUTE_TOKENS
    mod = pl.BlockSpec((1, 1, d), lambda i: ((i * tb) // rows_per_mod, 0, 0))
    slots = pl.BlockSpec((tb, N_SLOTS), lambda i: (i, 0))
    return pl.pallas_call(
        _route_kernel,
        grid=(t // tb,),
        in_specs=[pl.BlockSpec((tb, d), lambda i: (i, 0)), _full((1, d)), mod, mod,
                  _full((d, PEER_HEADS * PEER_D_KEY)), _full(k1.shape), _full(k2.shape)],
        out_specs=[pl.BlockSpec((tb, d // 128, 128), lambda i: (i, 0, 0)), slots, slots],
        out_shape=[jax.ShapeDtypeStruct((t, d // 128, 128), F32), jax.ShapeDtypeStruct((t, N_SLOTS), jnp.int32),
                   jax.ShapeDtypeStruct((t, N_SLOTS), F32)],
        scratch_shapes=[pltpu.VMEM((PEER_TOPK, tb), F32), pltpu.VMEM((PEER_TOPK, tb), F32),
                        pltpu.VMEM((PEER_TOPK, tb), F32), pltpu.VMEM((PEER_TOPK, tb), F32),
                        pltpu.VMEM((N_SLOTS, tb), jnp.int32), pltpu.VMEM((N_SLOTS, tb), F32)],
    )(h, gain.reshape(1, d), shift, scale, wq, k1, k2)


def _load_slabs(tab_ref, slot_idx, k):
    return jnp.concatenate([tab_ref[pl.ds(pl.multiple_of(slot_idx[k + j], SLAB), SLAB), :] for j in range(PAIR)], axis=0)


def _peer_u_kernel(idx_ref, x_ref, gate_ref, tab_ref, w_ref, p_ref, r_ref):
    tb = x_ref.shape[0]
    rows = N_SLOTS * SLAB

    ones = jnp.ones((8, 128), BF)
    for t in range(tb):
        x = x_ref[t]
        xlo = jnp.concatenate([x[0:SLAB]] * PAIR, axis=0)
        xhi = jnp.concatenate([x[SLAB:2 * SLAB]] * PAIR, axis=0)
        slot_idx = idx_ref.at[t]
        for k in range(0, N_SLOTS, PAIR):
            lo, hi = _unpack_words(_load_slabs(tab_ref, slot_idx, k))
            p_ref[pl.ds(t * rows + k * SLAB, PAIR * SLAB), :] = lo * xlo + hi * xhi
        part = p_ref[pl.ds(t * rows, N_SLOTS, stride=SLAB), :]
        for s in range(1, SLAB):
            part = part + p_ref[pl.ds(t * rows + s, N_SLOTS, stride=SLAB), :]
        hi = part.astype(BF)
        lo = (part - hi.astype(F32)).astype(BF)
        r_ref[t:t + 1, :] = (lax.dot_general(ones, hi, NT, preferred_element_type=F32)
                             + lax.dot_general(ones, lo, NT, preferred_element_type=F32))[0:1]
    r = r_ref[...]
    w_ref[...] = 0.5 * r * (1.0 + lax.erf(r * SQRT_HALF)) * gate_ref[...]


def _peer_v_kernel(idx_ref, w_ref, tab_ref, y_ref, wrep_ref):
    tb = y_ref.shape[0]
    n_acc = 4
    upper = lax.broadcasted_iota(jnp.int32, (PAIR * SLAB, 128), 0) >= SLAB
    for t in range(tb):
        wrep_ref[t] = jnp.broadcast_to(w_ref[t:t + 1, :], (N_SLOTS, 128)).T

    for t in range(tb):
        acc_lo = [jnp.zeros((PAIR * SLAB, 128), F32) for _ in range(n_acc)]
        acc_hi = [jnp.zeros((PAIR * SLAB, 128), F32) for _ in range(n_acc)]
        slot_idx = idx_ref.at[t]
        for k in range(0, N_SLOTS, PAIR):
            lo, hi = _unpack_words(_load_slabs(tab_ref, slot_idx, k))
            w = jnp.where(upper, wrep_ref[t, pl.ds(k + 1, 1), :], wrep_ref[t, pl.ds(k, 1), :])
            j = (k // PAIR) % n_acc
            acc_lo[j] = acc_lo[j] + w * lo
            acc_hi[j] = acc_hi[j] + w * hi
        lo = (acc_lo[0] + acc_lo[1]) + (acc_lo[2] + acc_lo[3])
        hi = (acc_hi[0] + acc_hi[1]) + (acc_hi[2] + acc_hi[3])
        lo = lo[0:SLAB] + lo[SLAB:2 * SLAB]
        hi = hi[0:SLAB] + hi[SLAB:2 * SLAB]
        for s in range(SLAB):
            y_ref[t:t + 1, s * 128:(s + 1) * 128] = lo[s:s + 1]
            y_ref[t:t + 1, HALF_D + s * 128:HALF_D + (s + 1) * 128] = hi[s:s + 1]


def _peer_experts(f, idx, gate, u_words, v_words):
    t = f.shape[0]
    d = f.shape[1] * f.shape[2]
    tb = PEER_TOKENS
    smem = pl.BlockSpec((tb, N_SLOTS), lambda i: (i, 0), memory_space=pltpu.SMEM)
    slots = pl.BlockSpec((tb, N_SLOTS), lambda i: (i, 0))
    resident = pl.BlockSpec(memory_space=pltpu.VMEM)
    rows3 = pl.BlockSpec((tb, 2 * SLAB, 128), lambda i: (i, 0, 0))
    params = pltpu.CompilerParams(vmem_limit_bytes=VMEM_TABLE_LIMIT)
    w = pl.pallas_call(
        _peer_u_kernel,
        grid=(t // tb,),
        in_specs=[smem, rows3, slots, resident],
        out_specs=slots,
        out_shape=jax.ShapeDtypeStruct((t, N_SLOTS), F32),
        scratch_shapes=[pltpu.VMEM((tb * N_SLOTS * SLAB, 128), F32), pltpu.VMEM((tb, N_SLOTS), F32)],
        compiler_params=params,
    )(idx, f, gate, u_words)
    return pl.pallas_call(
        _peer_v_kernel,
        grid=(t // tb,),
        in_specs=[smem, slots, resident],
        out_specs=pl.BlockSpec((tb, d), lambda i: (i, 0)),
        out_shape=jax.ShapeDtypeStruct((t, d), F32),
        scratch_shapes=[pltpu.VMEM((tb, N_SLOTS, 128), F32)],
        compiler_params=params,
    )(idx, w, v_words)


class _PeerWeights:
    def __init__(self, wq, keys1, keys2, u_tab, v_tab):
        half = PEER_D_KEY // 2
        self.wq = wq.astype(BF)
        self.k1 = jnp.pad(keys1, ((0, 0), (0, 0), (0, half))).astype(BF)
        self.k2 = jnp.pad(keys2, ((0, 0), (0, 0), (half, 0))).astype(BF)
        self.u = _pack_bf16_table(u_tab)
        self.v = _pack_bf16_table(v_tab)


def _peer_ffn(h, gain, shift, scale, rows_per_mod, pw):
    f, idx, gate = _peer_route(h, gain, shift, scale, rows_per_mod, pw.wq, pw.k1, pw.k2)
    return _peer_experts(f, idx, gate, pw.u, pw.v)


def _final_kernel(h_ref, y_ref, gate_ref, g_ref, o_ref):
    x = h_ref[...] + gate_ref[0] * y_ref[...]
    o_ref[...] = x * lax.rsqrt(jnp.mean(x * x, axis=-1, keepdims=True) + EPS) * g_ref[...]


def _final_norm(h, y, gate, gain, rows_per_mod):
    n, d = h.shape
    tm = ROW_TILE
    rows = pl.BlockSpec((tm, d), lambda i: (i, 0))
    mod = pl.BlockSpec((1, 1, d), lambda i: ((i * tm) // rows_per_mod, 0, 0))
    return pl.pallas_call(
        _final_kernel, grid=(n // tm,), in_specs=[rows, rows, mod, _full((1, d))], out_specs=rows,
        out_shape=jax.ShapeDtypeStruct((n, d), F32),
    )(h, y, gate, gain.reshape(1, d))


def kernel(x, c, ctx, c_ctx, ada_w, ada_b, mix_norm_g, ffn_norm_g, ev_w_in, ev_w_out,
           ret_decay_logit_f, ret_decay_logit_b, conv_w, od_w_in, od_w_out, attn_sinks,
           peer_wq, peer_keys1, peer_keys2, peer_u, peer_v, final_norm_g):
    batch, s, d = x.shape
    n_ctx = ctx.shape[1]
    assert DEPTH == 2 and s % ROW_TILE == 0 and n_ctx % ROW_TILE == 0 and batch + 1 <= ADA_ROWS
    h_lat = x.reshape(batch * s, d)
    h_ctx = ctx.reshape(batch * n_ctx, d)
    c_rows = jnp.zeros((ADA_ROWS, d), F32).at[:batch].set(c).at[batch].set(c_ctx)

    def modulation(layer):
        mod = _ada_modulation(c_rows, ada_w[layer], ada_b[layer])
        lat = [m.reshape(batch, 1, d) for m in jnp.split(mod[:batch], N_ADA, axis=-1)]
        cx = [m.reshape(1, 1, d) for m in jnp.split(mod[batch:batch + 1], N_ADA, axis=-1)]
        return lat, cx

    (sh1, sc1, g1, sh2, sc2, g2), (csh1, csc1, cg1, csh2, csc2, cg2) = modulation(0)
    w_in = ev_w_in[0]
    swap = _swap_columns(RET_HEADS, RET_QK_DIM)
    w0 = jnp.concatenate([w_in, w_in[:, :RET_Q_W][:, swap], w_in[:, RET_Q_W:2 * RET_Q_W][:, swap]], axis=1)
    p_lat = _in_projection(h_lat, mix_norm_g[0], sh1, sc1, w0, s)
    p_ctx = _in_projection(h_ctx, mix_norm_g[0], csh1, csc1, w0, batch * n_ctx)
    log_f = jax.nn.log_sigmoid(ret_decay_logit_f[0].astype(F32))
    log_b = jax.nn.log_sigmoid(ret_decay_logit_b[0].astype(F32))
    cos, sin = _rope_tables(s, RET_QK_DIM, RET_HEADS)
    ret_lat, ret_ctx = _retention(p_lat, p_ctx, log_f, log_b, cos, sin, batch)
    h_lat = _even_output(h_lat, g1, ret_lat, p_lat, conv_w[0], ev_w_out[0], s, s)
    h_ctx = _even_output(h_ctx, cg1, ret_ctx, p_ctx, conv_w[0], ev_w_out[0], n_ctx, batch * n_ctx)
    pw = _PeerWeights(peer_wq[0], peer_keys1[0], peer_keys2[0], peer_u[0], peer_v[0])
    y_lat = _peer_ffn(h_lat, ffn_norm_g[0], sh2, sc2, s, pw)
    y_ctx = _peer_ffn(h_ctx, ffn_norm_g[0], csh2, csc2, batch * n_ctx, pw)

    (sh1, sc1, g1, sh2, sc2, g2b), (csh1, csc1, _, _, _, _) = modulation(1)
    w_in = od_w_in[0]
    wq_cols, wk_cols, wv_cols = w_in[:, :ATT_Q_W], w_in[:, ATT_Q_W:ATT_Q_W + ATT_KV_W], w_in[:, ATT_Q_W + ATT_KV_W:]
    w1 = jnp.concatenate([wq_cols, wq_cols[:, _swap_columns(ATT_HEADS, ATT_HEAD_DIM)], wk_cols,
                          wk_cols[:, _swap_columns(ATT_KV_HEADS, ATT_HEAD_DIM)], wv_cols], axis=1)
    h_lat, p_lat = _in_projection(h_lat, mix_norm_g[1], sh1, sc1, w1, s, add=(y_lat, g2), emit_h=True)
    p_ctx = _in_projection(h_ctx, mix_norm_g[1], csh1, csc1, w_in[:, ATT_Q_W:], batch * n_ctx, add=(y_ctx, cg2))
    cos, sin = _rope_tables(s, ATT_HEAD_DIM, ATT_HEADS)
    h_lat = _attention(h_lat, g1, p_lat, p_ctx, attn_sinks[0].astype(F32), cos, sin, od_w_out[0], batch)
    pw = _PeerWeights(peer_wq[1], peer_keys1[1], peer_keys2[1], peer_u[1], peer_v[1])
    y_lat = _peer_ffn(h_lat, ffn_norm_g[1], sh2, sc2, s, pw)
    out = _final_norm(h_lat, y_lat, g2b, final_norm_g, s)
    return out.reshape(batch, s, d)
```

```python
import functools

import numpy as np
import jax
import jax.numpy as jnp
from jax import lax
from jax.experimental import pallas as pl
from jax.experimental.pallas import tpu as pltpu

D_MODEL = 1024
DEPTH = 2
GRID_W = 64
EPS = 1e-6
ROPE_BASE = 10000.0
NEG_INF = -1e30
N_ADA = 6
RET_HEADS = 4
RET_V_DIM = D_MODEL // (2 * RET_HEADS)
RET_QK_DIM = RET_V_DIM // 2
RET_CHUNK = 128
CONV_CH = D_MODEL // 2
CONV_K = 3
RET_Q_W = RET_HEADS * RET_QK_DIM
RET_V_W = RET_HEADS * RET_V_DIM
EV_COLS = 2 * RET_Q_W + 2 * RET_V_W + 3 * CONV_CH
ATT_HEADS = 16
ATT_HEAD_DIM = D_MODEL // ATT_HEADS
ATT_KV_HEADS = 4
ATT_GROUP = ATT_HEADS // ATT_KV_HEADS
WINDOW = 128
ATT_BLOCK = 128
ATT_Q_W = ATT_HEADS * ATT_HEAD_DIM
ATT_KV_W = ATT_KV_HEADS * ATT_HEAD_DIM
PEER_HEADS = 8
PEER_N_KEYS = 128
PEER_D_KEY = 128
PEER_TOPK = 16
ROUTE_TOKENS = 128
PEER_TOKENS = 64
PRODUCT_RING = 8
N_SLOTS = PEER_HEADS * PEER_TOPK
HALF_D = D_MODEL // 2
SLAB = HALF_D // 128
PAIR = 8 // SLAB
BF16_HI_MASK = -65536
VMEM_TABLE_LIMIT = 48 * 1024 * 1024
VMEM_MIXER_LIMIT = 40 * 1024 * 1024
ID_BIG = 1e9
SQRT_HALF = 0.7071067811865476
ROW_TILE = 256
ADA_ROWS = 40

NT = (((1,), (1,)), ((), ()))
TN = (((0,), (0,)), ((), ()))
BF = jnp.bfloat16
F32 = jnp.float32


def _full(shape):
    return pl.BlockSpec(shape, lambda *_: (0,) * len(shape))


def _mm(a, b):
    return jnp.dot(a.astype(BF), b.astype(BF), preferred_element_type=F32)


def _rope_tables(n_tok, head_dim, n_heads):
    n_rows = n_tok // GRID_W
    rows = jnp.broadcast_to(jnp.arange(n_rows, dtype=F32)[:, None], (n_rows, GRID_W)).reshape(-1)
    cols = jnp.broadcast_to(jnp.arange(GRID_W, dtype=F32)[None, :], (n_rows, GRID_W)).reshape(-1)
    n_freq = head_dim // 4
    inv_freq = ROPE_BASE ** (-jnp.arange(n_freq, dtype=F32) / n_freq)
    ar, ac = rows[:, None] * inv_freq, cols[:, None] * inv_freq
    cos = jnp.concatenate([jnp.cos(ar), jnp.cos(ar), jnp.cos(ac), jnp.cos(ac)], axis=-1)
    sin = jnp.concatenate([-jnp.sin(ar), jnp.sin(ar), -jnp.sin(ac), jnp.sin(ac)], axis=-1)
    return jnp.tile(cos, (1, n_heads)), jnp.tile(sin, (1, n_heads))


def _swap_columns(n_heads, head_dim):
    q = head_dim // 4
    base = np.concatenate([np.arange(q, 2 * q), np.arange(0, q), np.arange(3 * q, 4 * q), np.arange(2 * q, 3 * q)])
    return (np.arange(n_heads)[:, None] * head_dim + base[None, :]).reshape(-1)


def _ada_kernel(c_ref, w_ref, b_ref, o_ref):
    c = c_ref[...]
    s = c / (1.0 + jnp.exp(-c))
    o_ref[...] = _mm(s, w_ref[...]) + b_ref[...]


def _ada_modulation(c_rows, w, b):
    d, f = w.shape
    tn = f // 4
    return pl.pallas_call(
        _ada_kernel,
        grid=(f // tn,),
        in_specs=[_full((ADA_ROWS, d)), pl.BlockSpec((d, tn), lambda j: (0, j)), pl.BlockSpec((1, tn), lambda j: (0, j))],
        out_specs=pl.BlockSpec((ADA_ROWS, tn), lambda j: (0, j)),
        out_shape=jax.ShapeDtypeStruct((ADA_ROWS, f), F32),
    )(c_rows, w.astype(BF), b.reshape(1, f))


def _norm_mod(x, gain, shift, scale):
    n = x * lax.rsqrt(jnp.mean(x * x, axis=-1, keepdims=True) + EPS) * gain
    return n * (1.0 + scale) + shift


def _proj_kernel(*refs, has_add, emit_h):
    refs = list(refs)
    h_ref = refs.pop(0)
    x = h_ref[...]
    if has_add:
        y_ref, gt_ref = refs.pop(0), refs.pop(0)
        x = x + gt_ref[0] * y_ref[...]
    g_ref, sh_ref, sc_ref, w_ref = refs[:4]
    outs = refs[4:]
    if emit_h:
        outs.pop(0)[...] = x
    a = _norm_mod(x, g_ref[...], sh_ref[0], sc_ref[0])
    outs[0][...] = _mm(a, w_ref[...])


def _in_projection(h, gain, shift, scale, w, rows_per_mod, add=None, emit_h=False):
    n, d = h.shape
    f = w.shape[1]
    tm = ROW_TILE
    rows = pl.BlockSpec((tm, d), lambda i: (i, 0))
    mod = pl.BlockSpec((1, 1, d), lambda i: ((i * tm) // rows_per_mod, 0, 0))
    args, specs = [h], [rows]
    if add is not None:
        args += [add[0], add[1]]
        specs += [rows, mod]
    args += [gain.reshape(1, d), shift, scale, w.astype(BF)]
    specs += [_full((1, d)), mod, mod, _full((d, f))]
    out_shape = [jax.ShapeDtypeStruct((n, f), F32)]
    out_specs = [pl.BlockSpec((tm, f), lambda i: (i, 0))]
    if emit_h:
        out_shape.insert(0, jax.ShapeDtypeStruct((n, d), F32))
        out_specs.insert(0, rows)
    res = pl.pallas_call(
        functools.partial(_proj_kernel, has_add=add is not None, emit_h=emit_h),
        grid=(n // tm,), in_specs=specs, out_specs=out_specs, out_shape=out_shape,
        compiler_params=pltpu.CompilerParams(vmem_limit_bytes=VMEM_MIXER_LIMIT),
    )(*args)
    return res if emit_h else res[0]


def _retention_kernel(logf_ref, logb_ref, ql_ref, kl_ref, vl_ref, qs_ref, ks_ref, qc_ref, kc_ref, vc_ref,
                      cos_ref, sin_ref, ol_ref, oc_ref):
    c = RET_CHUNK
    n_lat = ql_ref.shape[0] // c
    n_ctx = qc_ref.shape[0] // c
    k_scale = RET_QK_DIM ** -0.5
    row = lax.broadcasted_iota(jnp.int32, (c, c), 0).astype(F32)
    col = lax.broadcasted_iota(jnp.int32, (c, c), 1).astype(F32)
    rowk = lax.broadcasted_iota(jnp.int32, (c, RET_QK_DIM), 0).astype(F32)

    def step(qh, kh, vh, state, dmat, xi, zeta, cdm):
        qb, vb = qh.astype(BF), vh.astype(BF)
        scores = lax.dot_general(qb, kh.astype(BF), NT, preferred_element_type=F32) * dmat
        out = _mm(scores, vb) + _mm(qb, state) * xi
        kv = lax.dot_general((kh * zeta).astype(BF), vb, TN, preferred_element_type=F32)
        return out, cdm * state + kv

    heads = range(RET_HEADS)
    qk = [slice(h * RET_QK_DIM, (h + 1) * RET_QK_DIM) for h in heads]
    vv = [slice(h * RET_V_DIM, (h + 1) * RET_V_DIM) for h in heads]
    for backward in (False, True):
        decay = []
        for h in heads:
            lg = (logb_ref if backward else logf_ref)[h]
            if backward:
                dmat = jnp.where(col >= row, jnp.exp(lg * (col - row)), 0.0)
                xi = jnp.exp(lg * (c - row))
                zeta = jnp.exp(lg * rowk)
            else:
                dmat = jnp.where(row >= col, jnp.exp(lg * (row - col)), 0.0)
                xi = jnp.exp(lg * (row + 1.0))
                zeta = jnp.exp(lg * (c - 1.0 - rowk))
            decay.append((dmat, xi, zeta, jnp.exp(jnp.full((RET_QK_DIM, RET_V_DIM), lg * c, F32))))
        states = [jnp.zeros((RET_QK_DIM, RET_V_DIM), F32) for _ in heads]
        for n in (range(n_ctx - 1, -1, -1) if backward else range(n_ctx)):
            r = slice(n * c, (n + 1) * c)
            for h in heads:
                out, states[h] = step(qc_ref[r, qk[h]], kc_ref[r, qk[h]] * k_scale, vc_ref[r, vv[h]], states[h], *decay[h])
                oc_ref[r, vv[h]] = oc_ref[r, vv[h]] + out if backward else out

        def lat_chunk(i, states):
            n = (n_lat - 1 - i) if backward else i
            r = pl.ds(pl.multiple_of(n * c, c), c)
            new_states = []
            for h in heads:
                cs, sn = cos_ref[r, qk[h]], sin_ref[r, qk[h]]
                qh = ql_ref[r, qk[h]] * cs + qs_ref[r, qk[h]] * sn
                kh = (kl_ref[r, qk[h]] * cs + ks_ref[r, qk[h]] * sn) * k_scale
                out, state = step(qh, kh, vl_ref[r, vv[h]], states[h], *decay[h])
                ol_ref[r, vv[h]] = ol_ref[r, vv[h]] + out if backward else out
                new_states.append(state)
            return tuple(new_states)

        lax.fori_loop(0, n_lat, lat_chunk, tuple(states))


def _retention(p_lat, p_ctx, log_f, log_b, cos, sin, batch):
    s = p_lat.shape[0] // batch
    c = p_ctx.shape[0] // batch
    qw = RET_Q_W
    swap0 = EV_COLS // qw
    smem = pl.BlockSpec(memory_space=pltpu.SMEM)
    lat = lambda width, j: pl.BlockSpec((s, width), lambda b: (b, j))
    ctx = lambda width, j: pl.BlockSpec((c, width), lambda b: (b, j))
    return pl.pallas_call(
        _retention_kernel,
        grid=(batch,),
        in_specs=[smem, smem, lat(qw, 0), lat(qw, 1), lat(RET_V_W, 1), lat(qw, swap0), lat(qw, swap0 + 1),
                  ctx(qw, 0), ctx(qw, 1), ctx(RET_V_W, 1), _full((s, qw)), _full((s, qw))],
        out_specs=[lat(RET_V_W, 0), ctx(RET_V_W, 0)],
        out_shape=[jax.ShapeDtypeStruct((batch * s, RET_V_W), F32), jax.ShapeDtypeStruct((batch * c, RET_V_W), F32)],
        compiler_params=pltpu.CompilerParams(vmem_limit_bytes=VMEM_MIXER_LIMIT),
    )(log_f, log_b, p_lat, p_lat, p_lat, p_lat, p_lat, p_ctx, p_ctx, p_ctx, cos, sin)


def _even_out_kernel(h_ref, gate_ref, ret_ref, g_ref, gb_ref, gc_ref, x_ref, gcp_ref, xp_ref, gcn_ref, xn_ref,
                     cw_ref, w_ref, o_ref, *, seq_blocks):
    i = pl.program_id(0)
    tm = h_ref.shape[0]
    u = gc_ref[...] * x_ref[...]
    seq_pos = i % seq_blocks
    keep_prev = jnp.where(seq_pos == 0, 0.0, 1.0)
    keep_next = jnp.where(seq_pos == seq_blocks - 1, 0.0, 1.0)
    u_prev = gcp_ref[7:8, :] * xp_ref[7:8, :] * keep_prev
    u_next = gcn_ref[0:1, :] * xn_ref[0:1, :] * keep_next
    rid = lax.broadcasted_iota(jnp.int32, u.shape, 0)
    up = jnp.where(rid == 0, u_prev, pltpu.roll(u, 1, 0))
    un = jnp.where(rid == tm - 1, u_next, pltpu.roll(u, tm - 1, 0))
    conv = cw_ref[0:1, :] * up + cw_ref[1:2, :] * u + cw_ref[2:3, :] * un
    parts = []
    for hh in range(RET_HEADS):
        lanes = slice(hh * RET_V_DIM, (hh + 1) * RET_V_DIM)
        r = ret_ref[:, lanes]
        g = g_ref[:, lanes]
        parts.append(r * lax.rsqrt(jnp.mean(r * r, axis=-1, keepdims=True) + EPS) * (g / (1.0 + jnp.exp(-g))))
    y = jnp.concatenate(parts + [gb_ref[...] * conv], axis=1)
    o_ref[...] = h_ref[...] + gate_ref[0] * _mm(y, w_ref[...])


def _even_output(h, gate, ret, p, conv_w, w_out, seq_len, rows_per_mod):
    n, d = h.shape
    tm = ROW_TILE
    cw = CONV_CH
    tiles = tm // 8
    last_tile = n // 8 - 1
    rows = lambda width, j: pl.BlockSpec((tm, width), lambda i: (i, j))
    prev = lambda j: pl.BlockSpec((8, cw), lambda i: (jnp.maximum(i * tiles - 1, 0), j))
    nxt = lambda j: pl.BlockSpec((8, cw), lambda i: (jnp.minimum((i + 1) * tiles, last_tile), j))
    mod = pl.BlockSpec((1, 1, d), lambda i: ((i * tm) // rows_per_mod, 0, 0))
    return pl.pallas_call(
        functools.partial(_even_out_kernel, seq_blocks=seq_len // tm),
        grid=(n // tm,),
        in_specs=[rows(d, 0), mod, rows(RET_V_W, 0), rows(cw, 2), rows(cw, 3), rows(cw, 4), rows(cw, 5),
                  prev(4), prev(5), nxt(4), nxt(5), _full((CONV_K, cw)), _full((d, d))],
        out_specs=rows(d, 0),
        out_shape=jax.ShapeDtypeStruct((n, d), F32),
    )(h, gate, ret, p, p, p, p, p, p, p, p, conv_w, w_out.astype(BF))


def _attn_kernel(sink_ref, h_ref, gate_ref, q_ref, qs_ref, kp_ref, kc_ref, kn_ref, ksp_ref, ksc_ref, ksn_ref,
                 vp_ref, vc_ref, vn_ref, kx_ref, vx_ref, cq_ref, sq_ref, ckp_ref, ckc_ref, ckn_ref,
                 skp_ref, skc_ref, skn_ref, w_ref, o_ref, *, n_lat):
    j = pl.program_id(1)
    blk = ATT_BLOCK
    span = blk + 2 * WINDOW
    n_keys = span + kx_ref.shape[0]
    scale = ATT_HEAD_DIM ** -0.5
    q = (q_ref[...] * cq_ref[...] + qs_ref[...] * sq_ref[...]) * scale
    keys = jnp.concatenate([kp_ref[...] * ckp_ref[...] + ksp_ref[...] * skp_ref[...],
                            kc_ref[...] * ckc_ref[...] + ksc_ref[...] * skc_ref[...],
                            kn_ref[...] * ckn_ref[...] + ksn_ref[...] * skn_ref[...],
                            kx_ref[...]], axis=0).astype(BF)
    vals = jnp.concatenate([vp_ref[...], vc_ref[...], vn_ref[...], vx_ref[...]], axis=0).astype(BF)
    qpos = lax.broadcasted_iota(jnp.int32, (blk, n_keys), 0)
    r = lax.broadcasted_iota(jnp.int32, (blk, n_keys), 1)
    key_pos = (j - 1) * blk + r
    in_band = (r >= qpos) & (r <= qpos + 2 * WINDOW) & (key_pos >= 0) & (key_pos < n_lat)
    valid = jnp.concatenate([in_band | (r >= span)] * ATT_GROUP, axis=0)
    heads = [None] * ATT_HEADS
    for kh in range(ATT_KV_HEADS):
        kv = slice(kh * ATT_HEAD_DIM, (kh + 1) * ATT_HEAD_DIM)
        qg = jnp.concatenate([q[:, (kh * ATT_GROUP + g) * ATT_HEAD_DIM:(kh * ATT_GROUP + g + 1) * ATT_HEAD_DIM]
                              for g in range(ATT_GROUP)], axis=0)
        s = lax.dot_general(qg.astype(BF), keys[:, kv], NT, preferred_element_type=F32)
        s = jnp.where(valid, s, NEG_INF)
        sink = jnp.concatenate([jnp.full((blk, 1), sink_ref[kh * ATT_GROUP + g], F32) for g in range(ATT_GROUP)], axis=0)
        m = jnp.maximum(jnp.max(s, axis=-1, keepdims=True), sink)
        p = jnp.exp(s - m)
        den = jnp.sum(p, axis=-1, keepdims=True) + jnp.exp(sink - m)
        o = _mm(p, vals[:, kv]) / den
        for g in range(ATT_GROUP):
            heads[kh * ATT_GROUP + g] = o[g * blk:(g + 1) * blk]
    o_ref[...] = h_ref[...] + gate_ref[0] * _mm(jnp.concatenate(heads, axis=1), w_ref[...])


def _attention(h, gate, p_lat, p_ctx, sinks, cos, sin, w_out, batch):
    n, d = h.shape
    s = n // batch
    c = p_ctx.shape[0] // batch
    blk = ATT_BLOCK
    nb = s // blk
    kw = ATT_KV_W
    k0 = 2 * ATT_Q_W // kw
    row = lambda width, col: pl.BlockSpec((blk, width), lambda b, j: (b * nb + j, col))
    prv = lambda width, col: pl.BlockSpec((blk, width), lambda b, j: (b * nb + jnp.maximum(j - 1, 0), col))
    nxt = lambda width, col: pl.BlockSpec((blk, width), lambda b, j: (b * nb + jnp.minimum(j + 1, nb - 1), col))
    tab = lambda width: pl.BlockSpec((blk, width), lambda b, j: (j, 0))
    tab_p = lambda width: pl.BlockSpec((blk, width), lambda b, j: (jnp.maximum(j - 1, 0), 0))
    tab_n = lambda width: pl.BlockSpec((blk, width), lambda b, j: (jnp.minimum(j + 1, nb - 1), 0))
    ctx = lambda col: pl.BlockSpec((c, kw), lambda b, j: (b, col))
    mod = pl.BlockSpec((1, 1, d), lambda b, j: (b, 0, 0))
    return pl.pallas_call(
        functools.partial(_attn_kernel, n_lat=s),
        grid=(batch, nb),
        in_specs=[pl.BlockSpec(memory_space=pltpu.SMEM), row(d, 0), mod, row(ATT_Q_W, 0), row(ATT_Q_W, 1),
                  prv(kw, k0), row(kw, k0), nxt(kw, k0), prv(kw, k0 + 1), row(kw, k0 + 1), nxt(kw, k0 + 1),
                  prv(kw, k0 + 2), row(kw, k0 + 2), nxt(kw, k0 + 2), ctx(0), ctx(1),
                  tab(ATT_Q_W), tab(ATT_Q_W), tab_p(kw), tab(kw), tab_n(kw), tab_p(kw), tab(kw), tab_n(kw),
                  pl.BlockSpec((d, d), lambda b, j: (0, 0))],
        out_specs=row(d, 0),
        out_shape=jax.ShapeDtypeStruct((n, d), F32),
    )(sinks, h, gate, p_lat, p_lat, p_lat, p_lat, p_lat, p_lat, p_lat, p_lat, p_lat, p_lat, p_lat, p_ctx, p_ctx,
      cos, sin, cos, cos, cos, sin, sin, sin, w_out.astype(BF))


def _pack_bf16_table(tab):
    e = tab.shape[0]
    bits = lax.bitcast_convert_type(tab.astype(BF), jnp.uint16).astype(jnp.uint32)
    word = (bits[:, HALF_D:] << 16) | bits[:, :HALF_D]
    return lax.bitcast_convert_type(word, jnp.int32).reshape(e * SLAB, 128)


def _unpack_words(words):
    lo = lax.bitcast_convert_type(words << 16, F32)
    hi = lax.bitcast_convert_type(words & BF16_HI_MASK, F32)
    return lo, hi


def _extract_top(s, order, n):
    vals, ids = [], []
    for _ in range(n):
        m = jnp.max(s, axis=0, keepdims=True)
        first = jnp.min(jnp.where(s == m, order, ID_BIG), axis=0, keepdims=True)
        vals.append(m)
        ids.append(first)
        s = jnp.where(order == first, -jnp.inf, s)
    return vals, ids


def _merge_sort_network(lo, hi):
    def merge(lo, hi, r):
        step = 2 * r
        if step < hi - lo:
            yield from merge(lo, hi, step)
            yield from merge(lo + r, hi, step)
            yield from [(i, i + r) for i in range(lo + r, hi - r, step)]
        else:
            yield (lo, lo + r)
    if hi > lo:
        mid = lo + (hi - lo) // 2
        yield from _merge_sort_network(lo, mid)
        yield from _merge_sort_network(mid + 1, hi)
        yield from merge(lo, hi, 1)


def _top_of_key_rows(s, key_id, n):
    depth = s.shape[0] // 8
    v = [s[8 * j:8 * j + 8, :] for j in range(depth)]
    ids = [key_id[8 * j:8 * j + 8, :] for j in range(depth)]
    for a, b in _merge_sort_network(0, depth - 1):
        a_first = (v[a] > v[b]) | ((v[a] == v[b]) & (ids[a] < ids[b]))
        v[a], v[b] = jnp.maximum(v[a], v[b]), jnp.minimum(v[a], v[b])
        ids[a], ids[b] = jnp.where(a_first, ids[a], ids[b]), jnp.where(a_first, ids[b], ids[a])
    out_v, out_i = [], []
    for r in range(n):
        live = min(depth, n - r)
        m = jnp.max(v[0], axis=0, keepdims=True)
        first = jnp.min(jnp.where(v[0] == m, ids[0], ID_BIG), axis=0, keepdims=True)
        out_v.append(m)
        out_i.append(first)
        if r + 1 < n:
            popped = ids[0] == first
            for j in range(live - 1):
                v[j] = jnp.where(popped, v[j + 1], v[j])
                ids[j] = jnp.where(popped, ids[j + 1], ids[j])
            v[live - 1] = jnp.where(popped, -jnp.inf, v[live - 1])
    return out_v, out_i


def _route_kernel(h_ref, g_ref, sh_ref, sc_ref, wq_ref, k1_ref, k2_ref, f_ref, idx_ref, gate_ref,
                  v1_ref, v2_ref, i1_ref, i2_ref, et_ref, gt_ref):
    tb = h_ref.shape[0]
    f = _norm_mod(h_ref[...], g_ref[...], sh_ref[0], sc_ref[0])
    for s in range(f_ref.shape[1]):
        f_ref[:, s, :] = f[:, s * 128:(s + 1) * 128]
    qb = _mm(f, wq_ref[...]).astype(BF)
    key_id = lax.broadcasted_iota(jnp.int32, (PEER_N_KEYS, tb), 0).astype(F32)
    r8 = lax.broadcasted_iota(jnp.int32, (8, tb), 0).astype(F32)
    flat = jnp.concatenate([r8 * 16, (r8 + 8) * 16, r8 * 16 + 1, r8 + 8, r8, r8 + 16, r8 + 32, r8 + 48, r8 + 64], axis=0)
    for h in range(PEER_HEADS):
        qh = qb[:, h * PEER_D_KEY:(h + 1) * PEER_D_KEY]
        s1 = lax.dot_general(k1_ref[h], qh, NT, preferred_element_type=F32)
        s2 = lax.dot_general(k2_ref[h], qh, NT, preferred_element_type=F32)
        for s, v_ref, i_ref in ((s1, v1_ref, i1_ref), (s2, v2_ref, i2_ref)):
            vals, ids = _top_of_key_rows(s, key_id, PEER_TOPK)
            for k in range(PEER_TOPK):
                v_ref[k:k + 1, :] = vals[k]
                i_ref[k:k + 1, :] = ids[k]
        v1a, v1b, v2a, v2b = v1_ref[0:8, :], v1_ref[8:16, :], v2_ref[0:8, :], v2_ref[8:16, :]
        i1a, i1b, i2a, i2b = i1_ref[0:8, :], i1_ref[8:16, :], i2_ref[0:8, :], i2_ref[8:16, :]
        ninf = -jnp.inf
        cand = jnp.concatenate([
            v1a + v2a[0:1], v1b + v2a[0:1], v1a + v2a[1:2], v2b + v1a[0:1],
            jnp.where(r8 >= 2, v2a + v1a[0:1], ninf),
            jnp.where(r8 >= 2, v2a + v1a[1:2], ninf),
            jnp.where((r8 >= 2) & (r8 <= 4), v2a + v1a[2:3], ninf),
            jnp.where((r8 >= 2) & (r8 <= 3), v2a + v1a[3:4], ninf),
            jnp.where(r8 == 2, v2a + v1a[4:5], ninf)], axis=0)
        expert = jnp.concatenate([
            i1a * PEER_N_KEYS + i2a[0:1], i1b * PEER_N_KEYS + i2a[0:1], i1a * PEER_N_KEYS + i2a[1:2],
            i1a[0:1] * PEER_N_KEYS + i2b,
            i1a[0:1] * PEER_N_KEYS + i2a, i1a[1:2] * PEER_N_KEYS + i2a, i1a[2:3] * PEER_N_KEYS + i2a,
            i1a[3:4] * PEER_N_KEYS + i2a, i1a[4:5] * PEER_N_KEYS + i2a], axis=0)
        cs, picks = _extract_top(cand, flat, PEER_TOPK)
        ex = [jnp.exp(c - cs[0]) for c in cs]
        den = ex[0]
        for e in ex[1:]:
            den = den + e
        for k in range(PEER_TOPK):
            row = h * PEER_TOPK + k
            pick = jnp.max(jnp.where(flat == picks[k], expert, -1.0), axis=0, keepdims=True)
            et_ref[row:row + 1, :] = (pick * SLAB).astype(jnp.int32)
            gt_ref[row:row + 1, :] = ex[k] / den
    idx_ref[...] = et_ref[...].T
    gate_ref[...] = gt_ref[...].T


def _peer_route(h, gain, shift, scale, rows_per_mod, wq, k1, k2):
    t, d = h.shape
    tb = ROUTE_TOKENS
    mod = pl.BlockSpec((1, 1, d), lambda i: ((i * tb) // rows_per_mod, 0, 0))
    slots = pl.BlockSpec((tb, N_SLOTS), lambda i: (i, 0))
    return pl.pallas_call(
        _route_kernel,
        grid=(t // tb,),
        in_specs=[pl.BlockSpec((tb, d), lambda i: (i, 0)), _full((1, d)), mod, mod,
                  _full((d, PEER_HEADS * PEER_D_KEY)), _full(k1.shape), _full(k2.shape)],
        out_specs=[pl.BlockSpec((tb, d // 128, 128), lambda i: (i, 0, 0)), slots, slots],
        out_shape=[jax.ShapeDtypeStruct((t, d // 128, 128), F32), jax.ShapeDtypeStruct((t, N_SLOTS), jnp.int32),
                   jax.ShapeDtypeStruct((t, N_SLOTS), F32)],
        scratch_shapes=[pltpu.VMEM((PEER_TOPK, tb), F32), pltpu.VMEM((PEER_TOPK, tb), F32),
                        pltpu.VMEM((PEER_TOPK, tb), F32), pltpu.VMEM((PEER_TOPK, tb), F32),
                        pltpu.VMEM((N_SLOTS, tb), jnp.int32), pltpu.VMEM((N_SLOTS, tb), F32)],
    )(h, gain.reshape(1, d), shift, scale, wq, k1, k2)


def _load_slabs(tab_ref, slot_idx, k):
    return jnp.concatenate([tab_ref[pl.ds(pl.multiple_of(slot_idx[k + j], SLAB), SLAB), :] for j in range(PAIR)], axis=0)


def _peer_u_kernel(idx_ref, x_ref, gate_ref, tab_ref, w_ref, p_ref, r_ref):
    tb = x_ref.shape[0]
    rows = N_SLOTS * SLAB

    ones = jnp.ones((8, 128), BF)
    for t in range(tb):
        x = x_ref[t]
        xlo = jnp.concatenate([x[0:SLAB]] * PAIR, axis=0)
        xhi = jnp.concatenate([x[SLAB:2 * SLAB]] * PAIR, axis=0)
        slot_idx = idx_ref.at[t]
        base = (t % PRODUCT_RING) * rows
        for k in range(0, N_SLOTS, PAIR):
            lo, hi = _unpack_words(_load_slabs(tab_ref, slot_idx, k))
            p_ref[pl.ds(base + k * SLAB, PAIR * SLAB), :] = lo * xlo + hi * xhi
        part = p_ref[pl.ds(base, N_SLOTS, stride=SLAB), :]
        for s in range(1, SLAB):
            part = part + p_ref[pl.ds(base + s, N_SLOTS, stride=SLAB), :]
        hi = part.astype(BF)
        lo = (part - hi.astype(F32)).astype(BF)
        r_ref[t:t + 1, :] = (lax.dot_general(ones, hi, NT, preferred_element_type=F32)
                             + lax.dot_general(ones, lo, NT, preferred_element_type=F32))[0:1]
    r = r_ref[...]
    w_ref[...] = 0.5 * r * (1.0 + lax.erf(r * SQRT_HALF)) * gate_ref[...]


def _peer_v_kernel(idx_ref, w_ref, tab_ref, y_ref, wrep_ref):
    tb = y_ref.shape[0]
    n_acc = 4
    upper = lax.broadcasted_iota(jnp.int32, (PAIR * SLAB, 128), 0) >= SLAB
    for t in range(tb):
        wrep_ref[t] = jnp.broadcast_to(w_ref[t:t + 1, :], (N_SLOTS, 128)).T

    for t in range(tb):
        acc_lo = [jnp.zeros((PAIR * SLAB, 128), F32) for _ in range(n_acc)]
        acc_hi = [jnp.zeros((PAIR * SLAB, 128), F32) for _ in range(n_acc)]
        slot_idx = idx_ref.at[t]
        for k in range(0, N_SLOTS, PAIR):
            lo, hi = _unpack_words(_load_slabs(tab_ref, slot_idx, k))
            w = jnp.where(upper, wrep_ref[t, pl.ds(k + 1, 1), :], wrep_ref[t, pl.ds(k, 1), :])
            j = (k // PAIR) % n_acc
            acc_lo[j] = acc_lo[j] + w * lo
            acc_hi[j] = acc_hi[j] + w * hi
        lo = (acc_lo[0] + acc_lo[1]) + (acc_lo[2] + acc_lo[3])
        hi = (acc_hi[0] + acc_hi[1]) + (acc_hi[2] + acc_hi[3])
        lo = lo[0:SLAB] + lo[SLAB:2 * SLAB]
        hi = hi[0:SLAB] + hi[SLAB:2 * SLAB]
        for s in range(SLAB):
            y_ref[t:t + 1, s * 128:(s + 1) * 128] = lo[s:s + 1]
            y_ref[t:t + 1, HALF_D + s * 128:HALF_D + (s + 1) * 128] = hi[s:s + 1]


def _peer_experts(f, idx, gate, u_words, v_words):
    t = f.shape[0]
    d = f.shape[1] * f.shape[2]
    tb = PEER_TOKENS
    smem = pl.BlockSpec((tb, N_SLOTS), lambda i: (i, 0), memory_space=pltpu.SMEM)
    slots = pl.BlockSpec((tb, N_SLOTS), lambda i: (i, 0))
    resident = pl.BlockSpec(memory_space=pltpu.VMEM)
    rows3 = pl.BlockSpec((tb, 2 * SLAB, 128), lambda i: (i, 0, 0))
    params = pltpu.CompilerParams(vmem_limit_bytes=VMEM_TABLE_LIMIT)
    w = pl.pallas_call(
        _peer_u_kernel,
        grid=(t // tb,),
        in_specs=[smem, rows3, slots, resident],
        out_specs=slots,
        out_shape=jax.ShapeDtypeStruct((t, N_SLOTS), F32),
        scratch_shapes=[pltpu.VMEM((PRODUCT_RING * N_SLOTS * SLAB, 128), F32), pltpu.VMEM((tb, N_SLOTS), F32)],
        compiler_params=params,
    )(idx, f, gate, u_words)
    return pl.pallas_call(
        _peer_v_kernel,
        grid=(t // tb,),
        in_specs=[smem, slots, resident],
        out_specs=pl.BlockSpec((tb, d), lambda i: (i, 0)),
        out_shape=jax.ShapeDtypeStruct((t, d), F32),
        scratch_shapes=[pltpu.VMEM((tb, N_SLOTS, 128), F32)],
        compiler_params=params,
    )(idx, w, v_words)


class _PeerWeights:
    def __init__(self, wq, keys1, keys2, u_tab, v_tab):
        half = PEER_D_KEY // 2
        self.wq = wq.astype(BF)
        self.k1 = jnp.pad(keys1, ((0, 0), (0, 0), (0, half))).astype(BF)
        self.k2 = jnp.pad(keys2, ((0, 0), (0, 0), (half, 0))).astype(BF)
        self.u = _pack_bf16_table(u_tab)
        self.v = _pack_bf16_table(v_tab)


def _peer_ffn(h, gain, shift, scale, rows_per_mod, pw):
    f, idx, gate = _peer_route(h, gain, shift, scale, rows_per_mod, pw.wq, pw.k1, pw.k2)
    return _peer_experts(f, idx, gate, pw.u, pw.v)


def _final_kernel(h_ref, y_ref, gate_ref, g_ref, o_ref):
    x = h_ref[...] + gate_ref[0] * y_ref[...]
    o_ref[...] = x * lax.rsqrt(jnp.mean(x * x, axis=-1, keepdims=True) + EPS) * g_ref[...]


def _final_norm(h, y, gate, gain, rows_per_mod):
    n, d = h.shape
    tm = ROW_TILE
    rows = pl.BlockSpec((tm, d), lambda i: (i, 0))
    mod = pl.BlockSpec((1, 1, d), lambda i: ((i * tm) // rows_per_mod, 0, 0))
    return pl.pallas_call(
        _final_kernel, grid=(n // tm,), in_specs=[rows, rows, mod, _full((1, d))], out_specs=rows,
        out_shape=jax.ShapeDtypeStruct((n, d), F32),
    )(h, y, gate, gain.reshape(1, d))


def kernel(x, c, ctx, c_ctx, ada_w, ada_b, mix_norm_g, ffn_norm_g, ev_w_in, ev_w_out,
           ret_decay_logit_f, ret_decay_logit_b, conv_w, od_w_in, od_w_out, attn_sinks,
           peer_wq, peer_keys1, peer_keys2, peer_u, peer_v, final_norm_g):
    batch, s, d = x.shape
    n_ctx = ctx.shape[1]
    assert DEPTH == 2 and s % ROW_TILE == 0 and n_ctx % ROW_TILE == 0 and batch + 1 <= ADA_ROWS
    h_lat = x.reshape(batch * s, d)
    h_ctx = ctx.reshape(batch * n_ctx, d)
    c_rows = jnp.zeros((ADA_ROWS, d), F32).at[:batch].set(c).at[batch].set(c_ctx)

    def modulation(layer):
        mod = _ada_modulation(c_rows, ada_w[layer], ada_b[layer])
        lat = [m.reshape(batch, 1, d) for m in jnp.split(mod[:batch], N_ADA, axis=-1)]
        cx = [m.reshape(1, 1, d) for m in jnp.split(mod[batch:batch + 1], N_ADA, axis=-1)]
        return lat, cx

    (sh1, sc1, g1, sh2, sc2, g2), (csh1, csc1, cg1, csh2, csc2, cg2) = modulation(0)
    w_in = ev_w_in[0]
    swap = _swap_columns(RET_HEADS, RET_QK_DIM)
    w0 = jnp.concatenate([w_in, w_in[:, :RET_Q_W][:, swap], w_in[:, RET_Q_W:2 * RET_Q_W][:, swap]], axis=1)
    p_lat = _in_projection(h_lat, mix_norm_g[0], sh1, sc1, w0, s)
    p_ctx = _in_projection(h_ctx, mix_norm_g[0], csh1, csc1, w0, batch * n_ctx)
    log_f = jax.nn.log_sigmoid(ret_decay_logit_f[0].astype(F32))
    log_b = jax.nn.log_sigmoid(ret_decay_logit_b[0].astype(F32))
    cos, sin = _rope_tables(s, RET_QK_DIM, RET_HEADS)
    ret_lat, ret_ctx = _retention(p_lat, p_ctx, log_f, log_b, cos, sin, batch)
    h_lat = _even_output(h_lat, g1, ret_lat, p_lat, conv_w[0], ev_w_out[0], s, s)
    h_ctx = _even_output(h_ctx, cg1, ret_ctx, p_ctx, conv_w[0], ev_w_out[0], n_ctx, batch * n_ctx)
    pw = _PeerWeights(peer_wq[0], peer_keys1[0], peer_keys2[0], peer_u[0], peer_v[0])
    y_lat = _peer_ffn(h_lat, ffn_norm_g[0], sh2, sc2, s, pw)
    y_ctx = _peer_ffn(h_ctx, ffn_norm_g[0], csh2, csc2, batch * n_ctx, pw)

    (sh1, sc1, g1, sh2, sc2, g2b), (csh1, csc1, _, _, _, _) = modulation(1)
    w_in = od_w_in[0]
    wq_cols, wk_cols, wv_cols = w_in[:, :ATT_Q_W], w_in[:, ATT_Q_W:ATT_Q_W + ATT_KV_W], w_in[:, ATT_Q_W + ATT_KV_W:]
    w1 = jnp.concatenate([wq_cols, wq_cols[:, _swap_columns(ATT_HEADS, ATT_HEAD_DIM)], wk_cols,
                          wk_cols[:, _swap_columns(ATT_KV_HEADS, ATT_HEAD_DIM)], wv_cols], axis=1)
    h_lat, p_lat = _in_projection(h_lat, mix_norm_g[1], sh1, sc1, w1, s, add=(y_lat, g2), emit_h=True)
    p_ctx = _in_projection(h_ctx, mix_norm_g[1], csh1, csc1, w_in[:, ATT_Q_W:], batch * n_ctx, add=(y_ctx, cg2))
    cos, sin = _rope_tables(s, ATT_HEAD_DIM, ATT_HEADS)
    h_lat = _attention(h_lat, g1, p_lat, p_ctx, attn_sinks[0].astype(F32), cos, sin, od_w_out[0], batch)
    pw = _PeerWeights(peer_wq[1], peer_keys1[1], peer_keys2[1], peer_u[1], peer_v[1])
    y_lat = _peer_ffn(h_lat, ffn_norm_g[1], sh2, sc2, s, pw)
    out = _final_norm(h_lat, y_lat, g2b, final_norm_g, s)
    return out.reshape(batch, s, d)
```

```python
import functools

import numpy as np
import jax
import jax.numpy as jnp
from jax import lax
from jax.experimental import pallas as pl
from jax.experimental.pallas import tpu as pltpu

D_MODEL = 1024
DEPTH = 2
GRID_W = 64
EPS = 1e-6
ROPE_BASE = 10000.0
NEG_INF = -1e30
N_ADA = 6
RET_HEADS = 4
RET_V_DIM = D_MODEL // (2 * RET_HEADS)
RET_QK_DIM = RET_V_DIM // 2
RET_CHUNK = 128
CONV_CH = D_MODEL // 2
CONV_K = 3
RET_Q_W = RET_HEADS * RET_QK_DIM
RET_V_W = RET_HEADS * RET_V_DIM
EV_COLS = 2 * RET_Q_W + 2 * RET_V_W + 3 * CONV_CH
ATT_HEADS = 16
ATT_HEAD_DIM = D_MODEL // ATT_HEADS
ATT_KV_HEADS = 4
ATT_GROUP = ATT_HEADS // ATT_KV_HEADS
WINDOW = 128
ATT_BLOCK = 128
ATT_Q_W = ATT_HEADS * ATT_HEAD_DIM
ATT_KV_W = ATT_KV_HEADS * ATT_HEAD_DIM
PEER_HEADS = 8
PEER_N_KEYS = 128
PEER_D_KEY = 128
PEER_TOPK = 16
ROUTE_TOKENS = 128
PEER_TOKENS = 64
PRODUCT_RING = 8
N_SLOTS = PEER_HEADS * PEER_TOPK
HALF_D = D_MODEL // 2
SLAB = HALF_D // 128
PAIR = 8 // SLAB
BF16_HI_MASK = -65536
VMEM_TABLE_LIMIT = 48 * 1024 * 1024
VMEM_MIXER_LIMIT = 40 * 1024 * 1024
ID_BIG = 1e9
SQRT_HALF = 0.7071067811865476
ROW_TILE = 256
ADA_ROWS = 40

NT = (((1,), (1,)), ((), ()))
TN = (((0,), (0,)), ((), ()))
BF = jnp.bfloat16
F32 = jnp.float32


def _full(shape):
    return pl.BlockSpec(shape, lambda *_: (0,) * len(shape))


def _mm(a, b):
    return jnp.dot(a.astype(BF), b.astype(BF), preferred_element_type=F32)


def _rope_tables(n_tok, head_dim, n_heads):
    n_rows = n_tok // GRID_W
    rows = jnp.broadcast_to(jnp.arange(n_rows, dtype=F32)[:, None], (n_rows, GRID_W)).reshape(-1)
    cols = jnp.broadcast_to(jnp.arange(GRID_W, dtype=F32)[None, :], (n_rows, GRID_W)).reshape(-1)
    n_freq = head_dim // 4
    inv_freq = ROPE_BASE ** (-jnp.arange(n_freq, dtype=F32) / n_freq)
    ar, ac = rows[:, None] * inv_freq, cols[:, None] * inv_freq
    cos = jnp.concatenate([jnp.cos(ar), jnp.cos(ar), jnp.cos(ac), jnp.cos(ac)], axis=-1)
    sin = jnp.concatenate([-jnp.sin(ar), jnp.sin(ar), -jnp.sin(ac), jnp.sin(ac)], axis=-1)
    return jnp.tile(cos, (1, n_heads)), jnp.tile(sin, (1, n_heads))


def _swap_columns(n_heads, head_dim):
    q = head_dim // 4
    base = np.concatenate([np.arange(q, 2 * q), np.arange(0, q), np.arange(3 * q, 4 * q), np.arange(2 * q, 3 * q)])
    return (np.arange(n_heads)[:, None] * head_dim + base[None, :]).reshape(-1)


def _ada_kernel(c_ref, w_ref, b_ref, o_ref):
    c = c_ref[...]
    s = c / (1.0 + jnp.exp(-c))
    o_ref[...] = _mm(s, w_ref[...]) + b_ref[...]


def _ada_modulation(c_rows, w, b):
    d, f = w.shape
    tn = f // 4
    return pl.pallas_call(
        _ada_kernel,
        grid=(f // tn,),
        in_specs=[_full((ADA_ROWS, d)), pl.BlockSpec((d, tn), lambda j: (0, j)), pl.BlockSpec((1, tn), lambda j: (0, j))],
        out_specs=pl.BlockSpec((ADA_ROWS, tn), lambda j: (0, j)),
        out_shape=jax.ShapeDtypeStruct((ADA_ROWS, f), F32),
    )(c_rows, w.astype(BF), b.reshape(1, f))


def _norm_mod(x, gain, shift, scale):
    n = x * lax.rsqrt(jnp.mean(x * x, axis=-1, keepdims=True) + EPS) * gain
    return n * (1.0 + scale) + shift


def _proj_kernel(*refs, has_add, emit_h):
    refs = list(refs)
    h_ref = refs.pop(0)
    x = h_ref[...]
    if has_add:
        y_ref, gt_ref = refs.pop(0), refs.pop(0)
        x = x + gt_ref[0] * y_ref[...]
    g_ref, sh_ref, sc_ref, w_ref = refs[:4]
    outs = refs[4:]
    if emit_h:
        outs.pop(0)[...] = x
    a = _norm_mod(x, g_ref[...], sh_ref[0], sc_ref[0])
    outs[0][...] = _mm(a, w_ref[...])


def _in_projection(h, gain, shift, scale, w, rows_per_mod, add=None, emit_h=False):
    n, d = h.shape
    f = w.shape[1]
    tm = ROW_TILE
    rows = pl.BlockSpec((tm, d), lambda i: (i, 0))
    mod = pl.BlockSpec((1, 1, d), lambda i: ((i * tm) // rows_per_mod, 0, 0))
    args, specs = [h], [rows]
    if add is not None:
        args += [add[0], add[1]]
        specs += [rows, mod]
    args += [gain.reshape(1, d), shift, scale, w.astype(BF)]
    specs += [_full((1, d)), mod, mod, _full((d, f))]
    out_shape = [jax.ShapeDtypeStruct((n, f), F32)]
    out_specs = [pl.BlockSpec((tm, f), lambda i: (i, 0))]
    if emit_h:
        out_shape.insert(0, jax.ShapeDtypeStruct((n, d), F32))
        out_specs.insert(0, rows)
    res = pl.pallas_call(
        functools.partial(_proj_kernel, has_add=add is not None, emit_h=emit_h),
        grid=(n // tm,), in_specs=specs, out_specs=out_specs, out_shape=out_shape,
        compiler_params=pltpu.CompilerParams(vmem_limit_bytes=VMEM_MIXER_LIMIT),
    )(*args)
    return res if emit_h else res[0]


def _retention_kernel(logf_ref, logb_ref, ql_ref, kl_ref, vl_ref, qs_ref, ks_ref, qc_ref, kc_ref, vc_ref,
                      cos_ref, sin_ref, ol_ref, oc_ref):
    c = RET_CHUNK
    n_lat = ql_ref.shape[0] // c
    n_ctx = qc_ref.shape[0] // c
    k_scale = RET_QK_DIM ** -0.5
    row = lax.broadcasted_iota(jnp.int32, (c, c), 0).astype(F32)
    col = lax.broadcasted_iota(jnp.int32, (c, c), 1).astype(F32)
    rowk = lax.broadcasted_iota(jnp.int32, (c, RET_QK_DIM), 0).astype(F32)

    def step(qh, kh, vh, state, dmat, xi, zeta, cdm):
        qb, vb = qh.astype(BF), vh.astype(BF)
        scores = lax.dot_general(qb, kh.astype(BF), NT, preferred_element_type=F32) * dmat
        out = _mm(scores, vb) + _mm(qb, state) * xi
        kv = lax.dot_general((kh * zeta).astype(BF), vb, TN, preferred_element_type=F32)
        return out, cdm * state + kv

    heads = range(RET_HEADS)
    qk = [slice(h * RET_QK_DIM, (h + 1) * RET_QK_DIM) for h in heads]
    vv = [slice(h * RET_V_DIM, (h + 1) * RET_V_DIM) for h in heads]
    for backward in (False, True):
        decay = []
        for h in heads:
            lg = (logb_ref if backward else logf_ref)[h]
            if backward:
                dmat = jnp.where(col >= row, jnp.exp(lg * (col - row)), 0.0)
                xi = jnp.exp(lg * (c - row))
                zeta = jnp.exp(lg * rowk)
            else:
                dmat = jnp.where(row >= col, jnp.exp(lg * (row - col)), 0.0)
                xi = jnp.exp(lg * (row + 1.0))
                zeta = jnp.exp(lg * (c - 1.0 - rowk))
            decay.append((dmat, xi, zeta, jnp.exp(jnp.full((RET_QK_DIM, RET_V_DIM), lg * c, F32))))
        states = [jnp.zeros((RET_QK_DIM, RET_V_DIM), F32) for _ in heads]
        for n in (range(n_ctx - 1, -1, -1) if backward else range(n_ctx)):
            r = slice(n * c, (n + 1) * c)
            for h in heads:
                out, states[h] = step(qc_ref[r, qk[h]], kc_ref[r, qk[h]] * k_scale, vc_ref[r, vv[h]], states[h], *decay[h])
                oc_ref[r, vv[h]] = oc_ref[r, vv[h]] + out if backward else out

        def lat_chunk(i, states):
            n = (n_lat - 1 - i) if backward else i
            r = pl.ds(pl.multiple_of(n * c, c), c)
            new_states = []
            for h in heads:
                cs, sn = cos_ref[r, qk[h]], sin_ref[r, qk[h]]
                qh = ql_ref[r, qk[h]] * cs + qs_ref[r, qk[h]] * sn
                kh = (kl_ref[r, qk[h]] * cs + ks_ref[r, qk[h]] * sn) * k_scale
                out, state = step(qh, kh, vl_ref[r, vv[h]], states[h], *decay[h])
                ol_ref[r, vv[h]] = ol_ref[r, vv[h]] + out if backward else out
                new_states.append(state)
            return tuple(new_states)

        lax.fori_loop(0, n_lat, lat_chunk, tuple(states))


def _retention(p_lat, p_ctx, log_f, log_b, cos, sin, batch):
    s = p_lat.shape[0] // batch
    c = p_ctx.shape[0] // batch
    qw = RET_Q_W
    swap0 = EV_COLS // qw
    smem = pl.BlockSpec(memory_space=pltpu.SMEM)
    lat = lambda width, j: pl.BlockSpec((s, width), lambda b: (b, j))
    ctx = lambda width, j: pl.BlockSpec((c, width), lambda b: (b, j))
    return pl.pallas_call(
        _retention_kernel,
        grid=(batch,),
        in_specs=[smem, smem, lat(qw, 0), lat(qw, 1), lat(RET_V_W, 1), lat(qw, swap0), lat(qw, swap0 + 1),
                  ctx(qw, 0), ctx(qw, 1), ctx(RET_V_W, 1), _full((s, qw)), _full((s, qw))],
        out_specs=[lat(RET_V_W, 0), ctx(RET_V_W, 0)],
        out_shape=[jax.ShapeDtypeStruct((batch * s, RET_V_W), F32), jax.ShapeDtypeStruct((batch * c, RET_V_W), F32)],
        compiler_params=pltpu.CompilerParams(vmem_limit_bytes=VMEM_MIXER_LIMIT),
    )(log_f, log_b, p_lat, p_lat, p_lat, p_lat, p_lat, p_ctx, p_ctx, p_ctx, cos, sin)


def _even_out_kernel(h_ref, gate_ref, ret_ref, g_ref, gb_ref, gc_ref, x_ref, gcp_ref, xp_ref, gcn_ref, xn_ref,
                     cw_ref, w_ref, o_ref, *, seq_blocks):
    i = pl.program_id(0)
    tm = h_ref.shape[0]
    u = gc_ref[...] * x_ref[...]
    seq_pos = i % seq_blocks
    keep_prev = jnp.where(seq_pos == 0, 0.0, 1.0)
    keep_next = jnp.where(seq_pos == seq_blocks - 1, 0.0, 1.0)
    u_prev = gcp_ref[7:8, :] * xp_ref[7:8, :] * keep_prev
    u_next = gcn_ref[0:1, :] * xn_ref[0:1, :] * keep_next
    rid = lax.broadcasted_iota(jnp.int32, u.shape, 0)
    up = jnp.where(rid == 0, u_prev, pltpu.roll(u, 1, 0))
    un = jnp.where(rid == tm - 1, u_next, pltpu.roll(u, tm - 1, 0))
    conv = cw_ref[0:1, :] * up + cw_ref[1:2, :] * u + cw_ref[2:3, :] * un
    parts = []
    for hh in range(RET_HEADS):
        lanes = slice(hh * RET_V_DIM, (hh + 1) * RET_V_DIM)
        r = ret_ref[:, lanes]
        g = g_ref[:, lanes]
        parts.append(r * lax.rsqrt(jnp.mean(r * r, axis=-1, keepdims=True) + EPS) * (g / (1.0 + jnp.exp(-g))))
    y = jnp.concatenate(parts + [gb_ref[...] * conv], axis=1)
    o_ref[...] = h_ref[...] + gate_ref[0] * _mm(y, w_ref[...])


def _even_output(h, gate, ret, p, conv_w, w_out, seq_len, rows_per_mod):
    n, d = h.shape
    tm = ROW_TILE
    cw = CONV_CH
    tiles = tm // 8
    last_tile = n // 8 - 1
    rows = lambda width, j: pl.BlockSpec((tm, width), lambda i: (i, j))
    prev = lambda j: pl.BlockSpec((8, cw), lambda i: (jnp.maximum(i * tiles - 1, 0), j))
    nxt = lambda j: pl.BlockSpec((8, cw), lambda i: (jnp.minimum((i + 1) * tiles, last_tile), j))
    mod = pl.BlockSpec((1, 1, d), lambda i: ((i * tm) // rows_per_mod, 0, 0))
    return pl.pallas_call(
        functools.partial(_even_out_kernel, seq_blocks=seq_len // tm),
        grid=(n // tm,),
        in_specs=[rows(d, 0), mod, rows(RET_V_W, 0), rows(cw, 2), rows(cw, 3), rows(cw, 4), rows(cw, 5),
                  prev(4), prev(5), nxt(4), nxt(5), _full((CONV_K, cw)), _full((d, d))],
        out_specs=rows(d, 0),
        out_shape=jax.ShapeDtypeStruct((n, d), F32),
    )(h, gate, ret, p, p, p, p, p, p, p, p, conv_w, w_out.astype(BF))


def _attn_kernel(sink_ref, h_ref, gate_ref, q_ref, qs_ref, kp_ref, kc_ref, kn_ref, ksp_ref, ksc_ref, ksn_ref,
                 vp_ref, vc_ref, vn_ref, kx_ref, vx_ref, cq_ref, sq_ref, ckp_ref, ckc_ref, ckn_ref,
                 skp_ref, skc_ref, skn_ref, w_ref, o_ref, *, n_lat):
    j = pl.program_id(1)
    blk = ATT_BLOCK
    span = blk + 2 * WINDOW
    n_keys = span + kx_ref.shape[0]
    scale = ATT_HEAD_DIM ** -0.5
    q = (q_ref[...] * cq_ref[...] + qs_ref[...] * sq_ref[...]) * scale
    keys = jnp.concatenate([kp_ref[...] * ckp_ref[...] + ksp_ref[...] * skp_ref[...],
                            kc_ref[...] * ckc_ref[...] + ksc_ref[...] * skc_ref[...],
                            kn_ref[...] * ckn_ref[...] + ksn_ref[...] * skn_ref[...],
                            kx_ref[...]], axis=0).astype(BF)
    vals = jnp.concatenate([vp_ref[...], vc_ref[...], vn_ref[...], vx_ref[...]], axis=0).astype(BF)
    qpos = lax.broadcasted_iota(jnp.int32, (blk, n_keys), 0)
    r = lax.broadcasted_iota(jnp.int32, (blk, n_keys), 1)
    key_pos = (j - 1) * blk + r
    in_band = (r >= qpos) & (r <= qpos + 2 * WINDOW) & (key_pos >= 0) & (key_pos < n_lat)
    valid = jnp.concatenate([in_band | (r >= span)] * ATT_GROUP, axis=0)
    heads = [None] * ATT_HEADS
    for kh in range(ATT_KV_HEADS):
        kv = slice(kh * ATT_HEAD_DIM, (kh + 1) * ATT_HEAD_DIM)
        qg = jnp.concatenate([q[:, (kh * ATT_GROUP + g) * ATT_HEAD_DIM:(kh * ATT_GROUP + g + 1) * ATT_HEAD_DIM]
                              for g in range(ATT_GROUP)], axis=0)
        s = lax.dot_general(qg.astype(BF), keys[:, kv], NT, preferred_element_type=F32)
        s = jnp.where(valid, s, NEG_INF)
        sink = jnp.concatenate([jnp.full((blk, 1), sink_ref[kh * ATT_GROUP + g], F32) for g in range(ATT_GROUP)], axis=0)
        m = jnp.maximum(jnp.max(s, axis=-1, keepdims=True), sink)
        p = jnp.exp(s - m)
        den = jnp.sum(p, axis=-1, keepdims=True) + jnp.exp(sink - m)
        o = _mm(p, vals[:, kv]) / den
        for g in range(ATT_GROUP):
            heads[kh * ATT_GROUP + g] = o[g * blk:(g + 1) * blk]
    o_ref[...] = h_ref[...] + gate_ref[0] * _mm(jnp.concatenate(heads, axis=1), w_ref[...])


def _attention(h, gate, p_lat, p_ctx, sinks, cos, sin, w_out, batch):
    n, d = h.shape
    s = n // batch
    c = p_ctx.shape[0] // batch
    blk = ATT_BLOCK
    nb = s // blk
    kw = ATT_KV_W
    k0 = 2 * ATT_Q_W // kw
    row = lambda width, col: pl.BlockSpec((blk, width), lambda b, j: (b * nb + j, col))
    prv = lambda width, col: pl.BlockSpec((blk, width), lambda b, j: (b * nb + jnp.maximum(j - 1, 0), col))
    nxt = lambda width, col: pl.BlockSpec((blk, width), lambda b, j: (b * nb + jnp.minimum(j + 1, nb - 1), col))
    tab = lambda width: pl.BlockSpec((blk, width), lambda b, j: (j, 0))
    tab_p = lambda width: pl.BlockSpec((blk, width), lambda b, j: (jnp.maximum(j - 1, 0), 0))
    tab_n = lambda width: pl.BlockSpec((blk, width), lambda b, j: (jnp.minimum(j + 1, nb - 1), 0))
    ctx = lambda col: pl.BlockSpec((c, kw), lambda b, j: (b, col))
    mod = pl.BlockSpec((1, 1, d), lambda b, j: (b, 0, 0))
    return pl.pallas_call(
        functools.partial(_attn_kernel, n_lat=s),
        grid=(batch, nb),
        in_specs=[pl.BlockSpec(memory_space=pltpu.SMEM), row(d, 0), mod, row(ATT_Q_W, 0), row(ATT_Q_W, 1),
                  prv(kw, k0), row(kw, k0), nxt(kw, k0), prv(kw, k0 + 1), row(kw, k0 + 1), nxt(kw, k0 + 1),
                  prv(kw, k0 + 2), row(kw, k0 + 2), nxt(kw, k0 + 2), ctx(0), ctx(1),
                  tab(ATT_Q_W), tab(ATT_Q_W), tab_p(kw), tab(kw), tab_n(kw), tab_p(kw), tab(kw), tab_n(kw),
                  pl.BlockSpec((d, d), lambda b, j: (0, 0))],
        out_specs=row(d, 0),
        out_shape=jax.ShapeDtypeStruct((n, d), F32),
    )(sinks, h, gate, p_lat, p_lat, p_lat, p_lat, p_lat, p_lat, p_lat, p_lat, p_lat, p_lat, p_lat, p_ctx, p_ctx,
      cos, sin, cos, cos, cos, sin, sin, sin, w_out.astype(BF))


def _pack_bf16_table(tab):
    e = tab.shape[0]
    bits = lax.bitcast_convert_type(tab.astype(BF), jnp.uint16).astype(jnp.uint32)
    word = (bits[:, HALF_D:] << 16) | bits[:, :HALF_D]
    return lax.bitcast_convert_type(word, jnp.int32).reshape(e * SLAB, 128)


def _unpack_words(words):
    lo = lax.bitcast_convert_type(words << 16, F32)
    hi = lax.bitcast_convert_type(words & BF16_HI_MASK, F32)
    return lo, hi


def _merge_sort_network(lo, hi):
    def merge(lo, hi, r):
        step = 2 * r
        if step < hi - lo:
            yield from merge(lo, hi, step)
            yield from merge(lo + r, hi, step)
            yield from [(i, i + r) for i in range(lo + r, hi - r, step)]
        else:
            yield (lo, lo + r)
    if hi > lo:
        mid = lo + (hi - lo) // 2
        yield from _merge_sort_network(lo, mid)
        yield from _merge_sort_network(mid + 1, hi)
        yield from merge(lo, hi, 1)


def _top_of_key_rows(s, key_id, n):
    depth = s.shape[0] // 8
    v = [s[8 * j:8 * j + 8, :] for j in range(depth)]
    ids = [key_id[8 * j:8 * j + 8, :] for j in range(depth)]
    for a, b in _merge_sort_network(0, depth - 1):
        a_first = (v[a] > v[b]) | ((v[a] == v[b]) & (ids[a] < ids[b]))
        v[a], v[b] = jnp.maximum(v[a], v[b]), jnp.minimum(v[a], v[b])
        ids[a], ids[b] = jnp.where(a_first, ids[a], ids[b]), jnp.where(a_first, ids[b], ids[a])
    out_v, out_i = [], []
    for r in range(n):
        live = min(depth, n - r)
        m = jnp.max(v[0], axis=0, keepdims=True)
        first = jnp.min(jnp.where(v[0] == m, ids[0], ID_BIG), axis=0, keepdims=True)
        out_v.append(m)
        out_i.append(first)
        if r + 1 < n:
            popped = ids[0] == first
            for j in range(live - 1):
                v[j] = jnp.where(popped, v[j + 1], v[j])
                ids[j] = jnp.where(popped, ids[j + 1], ids[j])
            v[live - 1] = jnp.where(popped, -jnp.inf, v[live - 1])
    return out_v, out_i


def _top_pair_sums(v1_ref, v2_ref, i1_ref, i2_ref, r8):
    n = PEER_TOPK
    ninf = -jnp.inf
    v1a, v1b, i1a, i1b = v1_ref[0:8, :], v1_ref[8:16, :], i1_ref[0:8, :] * PEER_N_KEYS, i1_ref[8:16, :] * PEER_N_KEYS
    sums, experts = [], []
    for b in range(n):
        tile = v1a + v2_ref[b:b + 1, :]
        lists = min(8, n // (b + 1))
        sums.append(tile if lists == 8 else jnp.where(r8 < lists, tile, ninf))
        experts.append(i1a + i2_ref[b:b + 1, :])
    tail, tail_expert = v1b + v2_ref[0:1, :], i1b + i2_ref[0:1, :]
    flat_tail = (r8 + 8.0) * n
    taken = jnp.zeros_like(r8)
    out_v, out_e = [], []
    for r in range(n):
        live = n - r
        m = jnp.max(jnp.maximum(sums[0], tail), axis=0, keepdims=True)
        flat = r8 * n + taken
        first = jnp.min(jnp.minimum(jnp.where(sums[0] == m, flat, ID_BIG), jnp.where(tail == m, flat_tail, ID_BIG)),
                        axis=0, keepdims=True)
        popped, popped_tail = flat == first, flat_tail == first
        out_v.append(m)
        out_e.append(jnp.max(jnp.maximum(jnp.where(popped, experts[0], -1.0), jnp.where(popped_tail, tail_expert, -1.0)),
                             axis=0, keepdims=True))
        if r + 1 < n:
            taken = jnp.where(popped, taken + 1.0, taken)
            for j in range(live - 1):
                sums[j] = jnp.where(popped, sums[j + 1], sums[j])
                experts[j] = jnp.where(popped, experts[j + 1], experts[j])
            sums[live - 1] = jnp.where(popped, ninf, sums[live - 1])
            tail = jnp.where(popped_tail, ninf, tail)
    return out_v, out_e


def _route_kernel(h_ref, g_ref, sh_ref, sc_ref, wq_ref, k1_ref, k2_ref, f_ref, idx_ref, gate_ref,
                  v1_ref, v2_ref, i1_ref, i2_ref, et_ref, gt_ref):
    tb = h_ref.shape[0]
    f = _norm_mod(h_ref[...], g_ref[...], sh_ref[0], sc_ref[0])
    for s in range(f_ref.shape[1]):
        f_ref[:, s, :] = f[:, s * 128:(s + 1) * 128]
    qb = _mm(f, wq_ref[...]).astype(BF)
    key_id = lax.broadcasted_iota(jnp.int32, (PEER_N_KEYS, tb), 0).astype(F32)
    r8 = lax.broadcasted_iota(jnp.int32, (8, tb), 0).astype(F32)
    for h in range(PEER_HEADS):
        qh = qb[:, h * PEER_D_KEY:(h + 1) * PEER_D_KEY]
        s1 = lax.dot_general(k1_ref[h], qh, NT, preferred_element_type=F32)
        s2 = lax.dot_general(k2_ref[h], qh, NT, preferred_element_type=F32)
        for s, v_ref, i_ref in ((s1, v1_ref, i1_ref), (s2, v2_ref, i2_ref)):
            vals, ids = _top_of_key_rows(s, key_id, PEER_TOPK)
            for k in range(PEER_TOPK):
                v_ref[k:k + 1, :] = vals[k]
                i_ref[k:k + 1, :] = ids[k]
        cs, picks = _top_pair_sums(v1_ref, v2_ref, i1_ref, i2_ref, r8)
        ex = [jnp.exp(c - cs[0]) for c in cs]
        den = ex[0]
        for e in ex[1:]:
            den = den + e
        for k in range(PEER_TOPK):
            row = h * PEER_TOPK + k
            et_ref[row:row + 1, :] = (picks[k] * SLAB).astype(jnp.int32)
            gt_ref[row:row + 1, :] = ex[k] / den
    idx_ref[...] = et_ref[...].T
    gate_ref[...] = gt_ref[...].T


def _peer_route(h, gain, shift, scale, rows_per_mod, wq, k1, k2):
    t, d = h.shape
    tb = ROUTE_TOKENS
    mod = pl.BlockSpec((1, 1, d), lambda i: ((i * tb) // rows_per_mod, 0, 0))
    slots = pl.BlockSpec((tb, N_SLOTS), lambda i: (i, 0))
    return pl.pallas_call(
        _route_kernel,
        grid=(t // tb,),
        in_specs=[pl.BlockSpec((tb, d), lambda i: (i, 0)), _full((1, d)), mod, mod,
                  _full((d, PEER_HEADS * PEER_D_KEY)), _full(k1.shape), _full(k2.shape)],
        out_specs=[pl.BlockSpec((tb, d // 128, 128), lambda i: (i, 0, 0)), slots, slots],
        out_shape=[jax.ShapeDtypeStruct((t, d // 128, 128), F32), jax.ShapeDtypeStruct((t, N_SLOTS), jnp.int32),
                   jax.ShapeDtypeStruct((t, N_SLOTS), F32)],
        scratch_shapes=[pltpu.VMEM((PEER_TOPK, tb), F32), pltpu.VMEM((PEER_TOPK, tb), F32),
                        pltpu.VMEM((PEER_TOPK, tb), F32), pltpu.VMEM((PEER_TOPK, tb), F32),
                        pltpu.VMEM((N_SLOTS, tb), jnp.int32), pltpu.VMEM((N_SLOTS, tb), F32)],
    )(h, gain.reshape(1, d), shift, scale, wq, k1, k2)


def _load_slabs(tab_ref, slot_idx, k):
    return jnp.concatenate([tab_ref[pl.ds(pl.multiple_of(slot_idx[k + j], SLAB), SLAB), :] for j in range(PAIR)], axis=0)


def _peer_u_kernel(idx_ref, x_ref, gate_ref, tab_ref, w_ref, p_ref, r_ref):
    tb = x_ref.shape[0]
    rows = N_SLOTS * SLAB

    ones = jnp.ones((8, 128), BF)
    for t in range(tb):
        x = x_ref[t]
        xlo = jnp.concatenate([x[0:SLAB]] * PAIR, axis=0)
        xhi = jnp.concatenate([x[SLAB:2 * SLAB]] * PAIR, axis=0)
        slot_idx = idx_ref.at[t]
        base = (t % PRODUCT_RING) * rows
        for k in range(0, N_SLOTS, PAIR):
            lo, hi = _unpack_words(_load_slabs(tab_ref, slot_idx, k))
            p_ref[pl.ds(base + k * SLAB, PAIR * SLAB), :] = lo * xlo + hi * xhi
        part = p_ref[pl.ds(base, N_SLOTS, stride=SLAB), :]
        for s in range(1, SLAB):
            part = part + p_ref[pl.ds(base + s, N_SLOTS, stride=SLAB), :]
        hi = part.astype(BF)
        lo = (part - hi.astype(F32)).astype(BF)
        r_ref[t:t + 1, :] = (lax.dot_general(ones, hi, NT, preferred_element_type=F32)
                             + lax.dot_general(ones, lo, NT, preferred_element_type=F32))[0:1]
    r = r_ref[...]
    w_ref[...] = 0.5 * r * (1.0 + lax.erf(r * SQRT_HALF)) * gate_ref[...]


def _peer_v_kernel(idx_ref, w_ref, tab_ref, y_ref, wrep_ref):
    tb = y_ref.shape[0]
    n_acc = 4
    upper = lax.broadcasted_iota(jnp.int32, (PAIR * SLAB, 128), 0) >= SLAB
    for t in range(tb):
        wrep_ref[t] = jnp.broadcast_to(w_ref[t:t + 1, :], (N_SLOTS, 128)).T

    for t in range(tb):
        acc_lo = [jnp.zeros((PAIR * SLAB, 128), F32) for _ in range(n_acc)]
        acc_hi = [jnp.zeros((PAIR * SLAB, 128), F32) for _ in range(n_acc)]
        slot_idx = idx_ref.at[t]
        for k in range(0, N_SLOTS, PAIR):
            lo, hi = _unpack_words(_load_slabs(tab_ref, slot_idx, k))
            w = jnp.where(upper, wrep_ref[t, pl.ds(k + 1, 1), :], wrep_ref[t, pl.ds(k, 1), :])
            j = (k // PAIR) % n_acc
            acc_lo[j] = acc_lo[j] + w * lo
            acc_hi[j] = acc_hi[j] + w * hi
        lo = (acc_lo[0] + acc_lo[1]) + (acc_lo[2] + acc_lo[3])
        hi = (acc_hi[0] + acc_hi[1]) + (acc_hi[2] + acc_hi[3])
        lo = lo[0:SLAB] + lo[SLAB:2 * SLAB]
        hi = hi[0:SLAB] + hi[SLAB:2 * SLAB]
        for s in range(SLAB):
            y_ref[t:t + 1, s * 128:(s + 1) * 128] = lo[s:s + 1]
            y_ref[t:t + 1, HALF_D + s * 128:HALF_D + (s + 1) * 128] = hi[s:s + 1]


def _peer_experts(f, idx, gate, u_words, v_words):
    t = f.shape[0]
    d = f.shape[1] * f.shape[2]
    tb = PEER_TOKENS
    smem = pl.BlockSpec((tb, N_SLOTS), lambda i: (i, 0), memory_space=pltpu.SMEM)
    slots = pl.BlockSpec((tb, N_SLOTS), lambda i: (i, 0))
    resident = pl.BlockSpec(memory_space=pltpu.VMEM)
    rows3 = pl.BlockSpec((tb, 2 * SLAB, 128), lambda i: (i, 0, 0))
    params = pltpu.CompilerParams(vmem_limit_bytes=VMEM_TABLE_LIMIT)
    w = pl.pallas_call(
        _peer_u_kernel,
        grid=(t // tb,),
        in_specs=[smem, rows3, slots, resident],
        out_specs=slots,
        out_shape=jax.ShapeDtypeStruct((t, N_SLOTS), F32),
        scratch_shapes=[pltpu.VMEM((PRODUCT_RING * N_SLOTS * SLAB, 128), F32), pltpu.VMEM((tb, N_SLOTS), F32)],
        compiler_params=params,
    )(idx, f, gate, u_words)
    return pl.pallas_call(
        _peer_v_kernel,
        grid=(t // tb,),
        in_specs=[smem, slots, resident],
        out_specs=pl.BlockSpec((tb, d), lambda i: (i, 0)),
        out_shape=jax.ShapeDtypeStruct((t, d), F32),
        scratch_shapes=[pltpu.VMEM((tb, N_SLOTS, 128), F32)],
        compiler_params=params,
    )(idx, w, v_words)


class _PeerWeights:
    def __init__(self, wq, keys1, keys2, u_tab, v_tab):
        half = PEER_D_KEY // 2
        self.wq = wq.astype(BF)
        self.k1 = jnp.pad(keys1, ((0, 0), (0, 0), (0, half))).astype(BF)
        self.k2 = jnp.pad(keys2, ((0, 0), (0, 0), (half, 0))).astype(BF)
        self.u = _pack_bf16_table(u_tab)
        self.v = _pack_bf16_table(v_tab)


def _peer_ffn(h, gain, shift, scale, rows_per_mod, pw):
    f, idx, gate = _peer_route(h, gain, shift, scale, rows_per_mod, pw.wq, pw.k1, pw.k2)
    return _peer_experts(f, idx, gate, pw.u, pw.v)


def _final_kernel(h_ref, y_ref, gate_ref, g_ref, o_ref):
    x = h_ref[...] + gate_ref[0] * y_ref[...]
    o_ref[...] = x * lax.rsqrt(jnp.mean(x * x, axis=-1, keepdims=True) + EPS) * g_ref[...]


def _final_norm(h, y, gate, gain, rows_per_mod):
    n, d = h.shape
    tm = ROW_TILE
    rows = pl.BlockSpec((tm, d), lambda i: (i, 0))
    mod = pl.BlockSpec((1, 1, d), lambda i: ((i * tm) // rows_per_mod, 0, 0))
    return pl.pallas_call(
        _final_kernel, grid=(n // tm,), in_specs=[rows, rows, mod, _full((1, d))], out_specs=rows,
        out_shape=jax.ShapeDtypeStruct((n, d), F32),
    )(h, y, gate, gain.reshape(1, d))


def kernel(x, c, ctx, c_ctx, ada_w, ada_b, mix_norm_g, ffn_norm_g, ev_w_in, ev_w_out,
           ret_decay_logit_f, ret_decay_logit_b, conv_w, od_w_in, od_w_out, attn_sinks,
           peer_wq, peer_keys1, peer_keys2, peer_u, peer_v, final_norm_g):
    batch, s, d = x.shape
    n_ctx = ctx.shape[1]
    assert DEPTH == 2 and s % ROW_TILE == 0 and n_ctx % ROW_TILE == 0 and batch + 1 <= ADA_ROWS
    h_lat = x.reshape(batch * s, d)
    h_ctx = ctx.reshape(batch * n_ctx, d)
    c_rows = jnp.zeros((ADA_ROWS, d), F32).at[:batch].set(c).at[batch].set(c_ctx)

    def modulation(layer):
        mod = _ada_modulation(c_rows, ada_w[layer], ada_b[layer])
        lat = [m.reshape(batch, 1, d) for m in jnp.split(mod[:batch], N_ADA, axis=-1)]
        cx = [m.reshape(1, 1, d) for m in jnp.split(mod[batch:batch + 1], N_ADA, axis=-1)]
        return lat, cx

    (sh1, sc1, g1, sh2, sc2, g2), (csh1, csc1, cg1, csh2, csc2, cg2) = modulation(0)
    w_in = ev_w_in[0]
    swap = _swap_columns(RET_HEADS, RET_QK_DIM)
    w0 = jnp.concatenate([w_in, w_in[:, :RET_Q_W][:, swap], w_in[:, RET_Q_W:2 * RET_Q_W][:, swap]], axis=1)
    p_lat = _in_projection(h_lat, mix_norm_g[0], sh1, sc1, w0, s)
    p_ctx = _in_projection(h_ctx, mix_norm_g[0], csh1, csc1, w0, batch * n_ctx)
    log_f = jax.nn.log_sigmoid(ret_decay_logit_f[0].astype(F32))
    log_b = jax.nn.log_sigmoid(ret_decay_logit_b[0].astype(F32))
    cos, sin = _rope_tables(s, RET_QK_DIM, RET_HEADS)
    ret_lat, ret_ctx = _retention(p_lat, p_ctx, log_f, log_b, cos, sin, batch)
    h_lat = _even_output(h_lat, g1, ret_lat, p_lat, conv_w[0], ev_w_out[0], s, s)
    h_ctx = _even_output(h_ctx, cg1, ret_ctx, p_ctx, conv_w[0], ev_w_out[0], n_ctx, batch * n_ctx)
    pw = _PeerWeights(peer_wq[0], peer_keys1[0], peer_keys2[0], peer_u[0], peer_v[0])
    y_lat = _peer_ffn(h_lat, ffn_norm_g[0], sh2, sc2, s, pw)
    y_ctx = _peer_ffn(h_ctx, ffn_norm_g[0], csh2, csc2, batch * n_ctx, pw)

    (sh1, sc1, g1, sh2, sc2, g2b), (csh1, csc1, _, _, _, _) = modulation(1)
    w_in = od_w_in[0]
    wq_cols, wk_cols, wv_cols = w_in[:, :ATT_Q_W], w_in[:, ATT_Q_W:ATT_Q_W + ATT_KV_W], w_in[:, ATT_Q_W + ATT_KV_W:]
    w1 = jnp.concatenate([wq_cols, wq_cols[:, _swap_columns(ATT_HEADS, ATT_HEAD_DIM)], wk_cols,
                          wk_cols[:, _swap_columns(ATT_KV_HEADS, ATT_HEAD_DIM)], wv_cols], axis=1)
    h_lat, p_lat = _in_projection(h_lat, mix_norm_g[1], sh1, sc1, w1, s, add=(y_lat, g2), emit_h=True)
    p_ctx = _in_projection(h_ctx, mix_norm_g[1], csh1, csc1, w_in[:, ATT_Q_W:], batch * n_ctx, add=(y_ctx, cg2))
    cos, sin = _rope_tables(s, ATT_HEAD_DIM, ATT_HEADS)
    h_lat = _attention(h_lat, g1, p_lat, p_ctx, attn_sinks[0].astype(F32), cos, sin, od_w_out[0], batch)
    pw = _PeerWeights(peer_wq[1], peer_keys1[1], peer_keys2[1], peer_u[1], peer_v[1])
    y_lat = _peer_ffn(h_lat, ffn_norm_g[1], sh2, sc2, s, pw)
    out = _final_norm(h_lat, y_lat, g2b, final_norm_g, s)
    return out.reshape(batch, s, d)
```
